```python
import jax
import jax.numpy as jnp
from jax import lax
import numpy as np


D_MODEL = 2048
BATCH = 8
SEQ = 4096
DEPTH = 4

GRID_W = 64
CTX_LEN = 256
N_MIXERS = 2
MLA_HEADS = 16
MLA_Q_RANK = 512
MLA_KV_RANK = 512
MLA_NOPE = 128
MLA_ROPE = 64
MLA_V = 128
RET_HEADS = 8
RET_DK = D_MODEL // RET_HEADS
RET_DV = 2 * D_MODEL // RET_HEADS
RET_CHUNK = 128
FFN_DIM = 256 * ((8 * D_MODEL // 3 + 255) // 256)
CONV_W = 3
N_MOD = 6
ROPE_BASE = 10000.0
NORM_EPS = 1e-6
GN_EPS = 1e-5
Q_BLOCK = 128

kernel_name = "hybrid_mla_retention_dit"


def rmsnorm(x, g):
    x32 = x.astype(jnp.float32)
    y = x32 * lax.rsqrt(jnp.mean(x32 * x32, axis=-1, keepdims=True) + NORM_EPS)
    return (y * g.astype(jnp.float32)).astype(x.dtype)


def rope(x, ang):
    m = x.shape[-1] // 2
    cos = jnp.cos(ang).astype(x.dtype)
    sin = jnp.sin(ang).astype(x.dtype)
    x1, x2 = x[..., :m], x[..., m:]
    return jnp.concatenate([x1 * cos - x2 * sin, x2 * cos + x1 * sin], axis=-1)


def axial_rope(x, ang_row, ang_col):
    half = x.shape[-1] // 2
    return jnp.concatenate([rope(x[..., :half], ang_row), rope(x[..., half:], ang_col)], axis=-1)


def merge_heads(y):
    B, H, L, d = y.shape
    return y.transpose(0, 2, 1, 3).reshape(B, L, H * d)


def attend(q, k, v):
    s = jnp.einsum('bhqd,bhkd->bhqk', q, k, preferred_element_type=jnp.float32) * (q.shape[-1] ** -0.5)
    p = jax.nn.softmax(s, axis=-1)
    return jnp.einsum('bhqk,bhkd->bhqd', p.astype(v.dtype), v)


def blocked_attend(q, k, v):
    B, H, L, dq = q.shape
    nb = L // Q_BLOCK
    qb = q.reshape(B, H, nb, Q_BLOCK, dq).transpose(2, 0, 1, 3, 4)
    ob = lax.map(lambda qi: attend(qi, k, v), qb)
    return ob.transpose(1, 2, 0, 3, 4).reshape(B, H, L, v.shape[-1])


def mla_q(h, w_dq, q_g, w_uq, ang_row, ang_col):
    B, L, _ = h.shape
    q = (rmsnorm(h @ w_dq, q_g) @ w_uq).reshape(B, L, MLA_HEADS, MLA_NOPE + MLA_ROPE).transpose(0, 2, 1, 3)
    q_nope, q_rope = q[..., :MLA_NOPE], q[..., MLA_NOPE:]
    if ang_row is not None:
        q_rope = axial_rope(q_rope, ang_row, ang_col)
    return jnp.concatenate([q_nope, q_rope], axis=-1)


def mla_kv(h, w_dkv, kv_g, w_ukv, ang_row, ang_col):
    B, L, _ = h.shape
    kv_in = h @ w_dkv
    c_kv = rmsnorm(kv_in[..., :MLA_KV_RANK], kv_g)
    k_rope = kv_in[..., MLA_KV_RANK:][:, None]
    if ang_row is not None:
        k_rope = axial_rope(k_rope, ang_row, ang_col)
    kv = (c_kv @ w_ukv).reshape(B, L, MLA_HEADS, MLA_NOPE + MLA_V).transpose(0, 2, 1, 3)
    k = jnp.concatenate([kv[..., :MLA_NOPE], jnp.broadcast_to(k_rope, (B, MLA_HEADS, L, MLA_ROPE))], axis=-1)
    return k, kv[..., MLA_NOPE:]


def mla_mixer(hx, hc, last, ang_row, ang_col, w_dq, q_g, w_uq, w_dkv, kv_g, w_ukv, w_o):
    qx = mla_q(hx, w_dq, q_g, w_uq, ang_row, ang_col)
    kx, vx = mla_kv(hx, w_dkv, kv_g, w_ukv, ang_row, ang_col)
    kc, vc = mla_kv(hc, w_dkv, kv_g, w_ukv, None, None)
    k_all = jnp.concatenate([kc, kx], axis=2)
    v_all = jnp.concatenate([vc, vx], axis=2)
    ox = merge_heads(blocked_attend(qx, k_all, v_all)) @ w_o
    if last:
        return ox, None
    qc = mla_q(hc, w_dq, q_g, w_uq, None, None)
    oc = merge_heads(attend(qc, kc, vc)) @ w_o
    return ox, oc


def ret_proj(h, w):
    B, L, _ = h.shape
    return (h @ w).reshape(B, L, RET_HEADS, -1).transpose(0, 2, 1, 3)


def retention_chunkwise(q, k, v, log_gamma, state0):
    B, H, L, _ = q.shape
    dv = v.shape[-1]
    n = L // RET_CHUNK
    pos = jnp.arange(RET_CHUNK, dtype=jnp.float32)
    lg = log_gamma[:, None]
    diff = pos[:, None] - pos[None, :]
    lower = diff >= 0
    intra = jnp.where(lower, jnp.exp(lg[:, :, None] * jnp.where(lower, diff, 0.0)), 0.0)
    xi = jnp.exp(lg * (pos + 1.0))[None, :, :, None]
    zeta = jnp.exp(lg * (RET_CHUNK - 1.0 - pos))[None, :, :, None]
    chunk_decay = jnp.exp(log_gamma * RET_CHUNK)[None, :, None, None]

    def to_chunks(t):
        return t.reshape(B, H, n, RET_CHUNK, t.shape[-1]).transpose(2, 0, 1, 3, 4).astype(jnp.float32)

    def step(state, inp):
        qi, ki, vi = inp
        s = jnp.einsum('bhid,bhjd->bhij', qi, ki) * intra
        y = jnp.einsum('bhij,bhjv->bhiv', s, vi) + jnp.einsum('bhid,bhdv->bhiv', qi, state) * xi
        state = state * chunk_decay + jnp.einsum('bhjd,bhjv->bhdv', ki * zeta, vi)
        return state, y

    state, ys = lax.scan(step, state0, (to_chunks(q), to_chunks(k), to_chunks(v)))
    return ys.transpose(1, 2, 0, 3, 4).reshape(B, H, L, dv), state


def retention_state(k, v, log_gamma):
    L = k.shape[2]
    w = jnp.exp(log_gamma[:, None] * (L - 1.0 - jnp.arange(L, dtype=jnp.float32)))
    return jnp.einsum('bhld,bhlv->bhdv', k.astype(jnp.float32) * w[None, :, :, None], v.astype(jnp.float32))


def head_norm(y):
    mu = jnp.mean(y, axis=-1, keepdims=True)
    var = jnp.mean(jnp.square(y - mu), axis=-1, keepdims=True)
    return (y - mu) * lax.rsqrt(var + GN_EPS)


def ret_output(h, yf, yb, w_gf, w_gb, w_o):
    nf = merge_heads(head_norm(yf)).astype(h.dtype)
    nb = merge_heads(head_norm(yb)).astype(h.dtype)
    return (jax.nn.silu(h @ w_gf) * nf + jax.nn.silu(h @ w_gb) * nb) @ w_o


def retention_mixer(hx, hc, last, ang_seq, w_q, w_k, w_v, w_gf, w_gb, w_o, decay_f, decay_b):
    B = hx.shape[0]
    lgf = -jnp.exp(decay_f.astype(jnp.float32))
    lgb = -jnp.exp(decay_b.astype(jnp.float32))
    k_scale = RET_DK ** -0.5
    flip = lambda t: jnp.flip(t, axis=2)
    qx = rope(ret_proj(hx, w_q), ang_seq)
    kx = rope(ret_proj(hx, w_k), ang_seq) * k_scale
    vx = ret_proj(hx, w_v)
    kc = ret_proj(hc, w_k) * k_scale
    vc = ret_proj(hc, w_v)
    if last:
        sf = retention_state(kc, vc, lgf)
        sb = retention_state(flip(kc), flip(vc), lgb)
        oc = None
    else:
        qc = ret_proj(hc, w_q)
        zero = jnp.zeros((B, RET_HEADS, RET_DK, RET_DV), jnp.float32)
        ycf, sf = retention_chunkwise(qc, kc, vc, lgf, zero)
        ycb, sb = retention_chunkwise(flip(qc), flip(kc), flip(vc), lgb, zero)
        oc = ret_output(hc, ycf, flip(ycb), w_gf, w_gb, w_o)
    yxf, _ = retention_chunkwise(qx, kx, vx, lgf, sf)
    yxb, _ = retention_chunkwise(flip(qx), flip(kx), flip(vx), lgb, sb)
    ox = ret_output(hx, yxf, flip(yxb), w_gf, w_gb, w_o)
    return ox, oc


def conv_ffn(h, w_gate, w_up, conv_w, conv_b, w_down):
    L = h.shape[1]
    g = h @ w_gate
    pad = CONV_W // 2
    gp = jnp.pad(g, ((0, 0), (pad, pad), (0, 0)))
    g = sum(gp[:, j:j + L] * conv_w[j] for j in range(CONV_W)) + conv_b
    return (jax.nn.silu(g) * (h @ w_up)) @ w_down


def setup_inputs(seed: int = 0) -> dict:
    key = jax.random.key(seed)
    keys = iter(jax.random.split(key, 64))
    f32 = jnp.float32
    D, F = D_MODEL, FFN_DIM
    n_a = len(range(0, DEPTH, N_MIXERS))
    n_b = len(range(1, DEPTH, N_MIXERS))

    def w(shape, fan_in, gain=1.0):
        return jax.random.normal(next(keys), shape, f32) * (gain * fan_in ** -0.5)

    def g(shape):
        return 1.0 + 0.02 * jax.random.normal(next(keys), shape, f32)

    def small(shape):
        return 0.01 * jax.random.normal(next(keys), shape, f32)

    heads = jnp.arange(RET_HEADS, dtype=f32)
    decay0 = jnp.log(-jnp.log(1.0 - 2.0 ** (-5.0 - heads)))
    return {
        "x": jax.random.normal(next(keys), (BATCH, SEQ, D), f32),
        "c": jax.random.normal(next(keys), (BATCH, D), f32),
        "ctx": jax.random.normal(next(keys), (BATCH, CTX_LEN, D), f32),
        "c_ctx": jax.random.normal(next(keys), (D,), f32),
        "mod_w": w((DEPTH, D, N_MOD * D), D, 0.5),
        "mod_b": small((DEPTH, N_MOD * D)),
        "norm_mix_g": g((DEPTH, D)),
        "norm_ffn_g": g((DEPTH, D)),
        "mla_w_dq": w((n_a, D, MLA_Q_RANK), D),
        "mla_q_norm_g": g((n_a, MLA_Q_RANK)),
        "mla_w_uq": w((n_a, MLA_Q_RANK, MLA_HEADS * (MLA_NOPE + MLA_ROPE)), MLA_Q_RANK),
        "mla_w_dkv": w((n_a, D, MLA_KV_RANK + MLA_ROPE), D),
        "mla_kv_norm_g": g((n_a, MLA_KV_RANK)),
        "mla_w_ukv": w((n_a, MLA_KV_RANK, MLA_HEADS * (MLA_NOPE + MLA_V)), MLA_KV_RANK),
        "mla_w_o": w((n_a, MLA_HEADS * MLA_V, D), MLA_HEADS * MLA_V),
        "ret_w_q": w((n_b, D, RET_HEADS * RET_DK), D),
        "ret_w_k": w((n_b, D, RET_HEADS * RET_DK), D),
        "ret_w_v": w((n_b, D, RET_HEADS * RET_DV), D),
        "ret_w_gf": w((n_b, D, RET_HEADS * RET_DV), D),
        "ret_w_gb": w((n_b, D, RET_HEADS * RET_DV), D),
        "ret_w_o": w((n_b, RET_HEADS * RET_DV, D), RET_HEADS * RET_DV),
        "ret_decay_f": decay0[None] + 0.1 * jax.random.normal(next(keys), (n_b, RET_HEADS), f32),
        "ret_decay_b": decay0[None] + 0.1 * jax.random.normal(next(keys), (n_b, RET_HEADS), f32),
        "ffn_w_gate": w((DEPTH, D, F), D),
        "ffn_w_up": w((DEPTH, D, F), D),
        "ffn_conv_w": w((DEPTH, CONV_W, F), CONV_W),
        "ffn_conv_b": small((DEPTH, F)),
        "ffn_w_down": w((DEPTH, F, D), F),
        "final_norm_g": g((D,)),
    }


def reference(x, c, ctx, c_ctx, mod_w, mod_b, norm_mix_g, norm_ffn_g,
              mla_w_dq, mla_q_norm_g, mla_w_uq, mla_w_dkv, mla_kv_norm_g, mla_w_ukv, mla_w_o,
              ret_w_q, ret_w_k, ret_w_v, ret_w_gf, ret_w_gb, ret_w_o, ret_decay_f, ret_decay_b,
              ffn_w_gate, ffn_w_up, ffn_conv_w, ffn_conv_b, ffn_w_down, final_norm_g):
    L = x.shape[1]
    rows = L // GRID_W
    f32 = jnp.float32
    row = jnp.repeat(jnp.arange(rows), GRID_W).astype(f32)
    col = jnp.tile(jnp.arange(GRID_W), rows).astype(f32)
    axis_dim = MLA_ROPE // 2
    inv_axis = ROPE_BASE ** (-jnp.arange(0, axis_dim, 2, dtype=f32) / axis_dim)
    ang_row = row[:, None] * inv_axis[None]
    ang_col = col[:, None] * inv_axis[None]
    inv_ret = ROPE_BASE ** (-jnp.arange(0, RET_DK, 2, dtype=f32) / RET_DK)
    ang_seq = jnp.arange(L, dtype=f32)[:, None] * inv_ret[None]

    cond_x = jax.nn.silu(c)
    cond_c = jax.nn.silu(c_ctx)[None]
    for l in range(DEPTH):
        last = l == DEPTH - 1
        i = l // N_MIXERS
        mod_x = (cond_x @ mod_w[l] + mod_b[l])[:, None, :]
        mod_c = (cond_c @ mod_w[l] + mod_b[l])[:, None, :]
        sh1x, sc1x, g1x, sh2x, sc2x, g2x = jnp.split(mod_x, N_MOD, axis=-1)
        sh1c, sc1c, g1c, sh2c, sc2c, g2c = jnp.split(mod_c, N_MOD, axis=-1)
        hx = rmsnorm(x, norm_mix_g[l]) * (1.0 + sc1x) + sh1x
        hc = rmsnorm(ctx, norm_mix_g[l]) * (1.0 + sc1c) + sh1c
        if l % N_MIXERS == 0:
            ox, oc = mla_mixer(hx, hc, last, ang_row, ang_col, mla_w_dq[i], mla_q_norm_g[i], mla_w_uq[i],
                               mla_w_dkv[i], mla_kv_norm_g[i], mla_w_ukv[i], mla_w_o[i])
        else:
            ox, oc = retention_mixer(hx, hc, last, ang_seq, ret_w_q[i], ret_w_k[i], ret_w_v[i],
                                     ret_w_gf[i], ret_w_gb[i], ret_w_o[i], ret_decay_f[i], ret_decay_b[i])
        x = x + g1x * ox
        hx2 = rmsnorm(x, norm_ffn_g[l]) * (1.0 + sc2x) + sh2x
        x = x + g2x * conv_ffn(hx2, ffn_w_gate[l], ffn_w_up[l], ffn_conv_w[l], ffn_conv_b[l], ffn_w_down[l])
        if not last:
            ctx = ctx + g1c * oc
            hc2 = rmsnorm(ctx, norm_ffn_g[l]) * (1.0 + sc2c) + sh2c
            ctx = ctx + g2c * conv_ffn(hc2, ffn_w_gate[l], ffn_w_up[l], ffn_conv_w[l], ffn_conv_b[l], ffn_w_down[l])
    return rmsnorm(x, final_norm_g)
```

```python
import functools

import jax
import jax.numpy as jnp
from jax import lax
from jax.experimental import pallas as pl
from jax.experimental.pallas import tpu as pltpu

GRID_W = 64
N_MIXERS = 2
MLA_HEADS = 16
MLA_NOPE = 128
MLA_ROPE = 64
MLA_V = 128
RET_HEADS = 8
RET_CHUNK = 128
CONV_W = 3
N_MOD = 6
ROPE_BASE = 10000.0
NORM_EPS = 1e-6
GN_EPS = 1e-5

V7X_VMEM_BYTES = 64 * 1024 * 1024
LANES = 128
BF16_SUBLANES = 16
MXU_COL = 256

F32 = jnp.float32
BF16 = jnp.bfloat16


def _vmem_limit(block_bytes, scratch_bytes=0, temp_bytes=0):
    need = 2 * block_bytes + scratch_bytes + temp_bytes + (4 << 20)
    return int(min(need, V7X_VMEM_BYTES - (6 << 20)))


def _params(sem, limit):
    return pltpu.CompilerParams(dimension_semantics=sem, vmem_limit_bytes=limit)


def _pick_tile(n, prefs):
    for t in prefs:
        if n % t == 0:
            return t
    raise ValueError(f"no tile in {prefs} divides {n}")


def _silu(x):
    return x * (1.0 / (1.0 + jnp.exp(-x)))


def _mod_kernel(a_ref, w_ref, b_ref, o_ref):
    a = _silu(a_ref[...]).astype(BF16)
    acc = jnp.dot(a, w_ref[0].astype(BF16), preferred_element_type=F32)
    o_ref[0] = acc + b_ref[0]


def _modulation(cond, mod_w, mod_b):
    depth, d, n = mod_w.shape
    rows = cond.shape[0]
    tn = _pick_tile(n, (1024, 512, 256, 128))
    return pl.pallas_call(
        _mod_kernel,
        grid=(depth, n // tn),
        in_specs=[
            pl.BlockSpec((rows, d), lambda l, j: (0, 0)),
            pl.BlockSpec((1, d, tn), lambda l, j: (l, 0, j)),
            pl.BlockSpec((1, 1, tn), lambda l, j: (l, 0, j)),
        ],
        out_specs=pl.BlockSpec((1, rows, tn), lambda l, j: (l, 0, j)),
        out_shape=jax.ShapeDtypeStruct((depth, rows, n), F32),
        compiler_params=_params(("arbitrary", "arbitrary"),
                                _vmem_limit(d * tn * 4 + rows * (d + tn) * 4, 0, d * tn * 2)),
    )(cond, mod_w, mod_b.reshape(depth, 1, n))


def _norm_mod_kernel(x_ref, g_ref, sc_ref, sh_ref, o_ref):
    x = x_ref[...]
    y = x * lax.rsqrt(jnp.mean(x * x, axis=-1, keepdims=True) + NORM_EPS) * g_ref[...]
    o_ref[...] = (y * (1.0 + sc_ref[0]) + sh_ref[0]).astype(o_ref.dtype)


def _norm_mod(xc, g, sc, sh, sub):
    m, d = xc.shape
    return pl.pallas_call(
        _norm_mod_kernel,
        grid=(m // sub,),
        in_specs=[
            pl.BlockSpec((sub, d), lambda i: (i, 0)),
            pl.BlockSpec((1, d), lambda i: (0, 0)),
            pl.BlockSpec((1, 1, d), lambda i: (i, 0, 0)),
            pl.BlockSpec((1, 1, d), lambda i: (i, 0, 0)),
        ],
        out_specs=pl.BlockSpec((sub, d), lambda i: (i, 0)),
        out_shape=jax.ShapeDtypeStruct((m, d), BF16),
        compiler_params=_params(("arbitrary",), _vmem_limit(sub * d * 6, 0, sub * d * 8)),
    )(xc, g.reshape(1, d), sc, sh)


def _final_norm_kernel(x_ref, g_ref, o_ref):
    x = x_ref[...]
    o_ref[...] = x * lax.rsqrt(jnp.mean(x * x, axis=-1, keepdims=True) + NORM_EPS) * g_ref[...]


def _final_norm(xc, g, batch, seq, sub):
    m, d = xc.shape
    nsb = m // batch // sub
    nx = seq // sub
    return pl.pallas_call(
        _final_norm_kernel,
        grid=(batch, nx),
        in_specs=[
            pl.BlockSpec((sub, d), lambda b, j: (b * nsb + 1 + j, 0)),
            pl.BlockSpec((1, d), lambda b, j: (0, 0)),
        ],
        out_specs=pl.BlockSpec((sub, d), lambda b, j: (b * nx + j, 0)),
        out_shape=jax.ShapeDtypeStruct((batch * seq, d), F32),
        compiler_params=_params(("arbitrary", "arbitrary"), _vmem_limit(sub * d * 8, 0, sub * d * 8)),
    )(xc, g.reshape(1, d))


def _mm_kernel(a_ref, w_ref, o_ref, *, act):
    acc = jnp.dot(a_ref[...], w_ref[...], preferred_element_type=F32)
    if act == "silu":
        acc = _silu(acc)
    o_ref[...] = acc.astype(o_ref.dtype)


def _matmul(a, w, out_dtype, act=None):
    m, k = a.shape
    n = w.shape[1]
    tm = _pick_tile(m, (1024, 512, 256))
    tn = _pick_tile(n, (1024, 512, 256, 128))
    osz = jnp.dtype(out_dtype).itemsize
    return pl.pallas_call(
        functools.partial(_mm_kernel, act=act),
        grid=(m // tm, n // tn),
        in_specs=[
            pl.BlockSpec((tm, k), lambda i, j: (i, 0)),
            pl.BlockSpec((k, tn), lambda i, j: (0, j)),
        ],
        out_specs=pl.BlockSpec((tm, tn), lambda i, j: (i, j)),
        out_shape=jax.ShapeDtypeStruct((m, n), out_dtype),
        compiler_params=_params(("arbitrary", "arbitrary"),
                                _vmem_limit(tm * k * 2 + k * tn * 2 + tm * tn * osz, 0, tm * tn * 8)),
    )(a, w)


def _mm_res_kernel(a_ref, w_ref, res_ref, gate_ref, o_ref, *, sub):
    acc = jnp.dot(a_ref[...], w_ref[...], preferred_element_type=F32)
    for s in range(acc.shape[0] // sub):
        rows = slice(s * sub, (s + 1) * sub)
        o_ref[rows, :] = res_ref[rows, :] + gate_ref[s] * acc[rows, :]


def _matmul_gated_residual(a, w, res, gate, sub):
    m, k = a.shape
    n = w.shape[1]
    tm = _pick_tile(m, (512, 256))
    tn = _pick_tile(n, (1024, 512, 256, 128))
    return pl.pallas_call(
        functools.partial(_mm_res_kernel, sub=sub),
        grid=(m // tm, n // tn),
        in_specs=[
            pl.BlockSpec((tm, k), lambda i, j: (i, 0)),
            pl.BlockSpec((k, tn), lambda i, j: (0, j)),
            pl.BlockSpec((tm, tn), lambda i, j: (i, j)),
            pl.BlockSpec((tm // sub, 1, tn), lambda i, j: (i, 0, j)),
        ],
        out_specs=pl.BlockSpec((tm, tn), lambda i, j: (i, j)),
        out_shape=jax.ShapeDtypeStruct((m, n), F32),
        compiler_params=_params(("arbitrary", "arbitrary"),
                                _vmem_limit(tm * k * 2 + k * tn * 2 + tm * tn * 8, 0, tm * tn * 8)),
    )(a, w, res, gate)


def _rms(x, g):
    return x * lax.rsqrt(jnp.mean(x * x, axis=-1, keepdims=True) + NORM_EPS) * g


def _rope_pair(r, cos, sin):
    half = r.shape[-1] // 2
    return r * cos + pltpu.roll(r, half, 1) * sin


def _mla_down_kernel(a_ref, w_ref, qg_ref, kvg_ref, cos_ref, sin_ref, cq_ref, ckv_ref, kr_ref, *, rq, rkv):
    acc = jnp.dot(a_ref[...], w_ref[...], preferred_element_type=F32)
    cq_ref[...] = _rms(acc[:, :rq], qg_ref[...]).astype(cq_ref.dtype)
    ckv_ref[...] = _rms(acc[:, rq:rq + rkv], kvg_ref[...]).astype(ckv_ref.dtype)
    kr_ref[...] = _rope_pair(acc[:, rq + rkv:], cos_ref[...], sin_ref[...]).astype(kr_ref.dtype)


def _mla_down(h, w_cat, q_g, kv_g, cos, sin, rq, rkv):
    m, k = h.shape
    n = w_cat.shape[1]
    nr = n - rq - rkv
    tm = _pick_tile(m, (1024, 512, 256))
    return pl.pallas_call(
        functools.partial(_mla_down_kernel, rq=rq, rkv=rkv),
        grid=(m // tm,),
        in_specs=[
            pl.BlockSpec((tm, k), lambda i: (i, 0)),
            pl.BlockSpec((k, n), lambda i: (0, 0)),
            pl.BlockSpec((1, rq), lambda i: (0, 0)),
            pl.BlockSpec((1, rkv), lambda i: (0, 0)),
            pl.BlockSpec((tm, nr), lambda i: (i, 0)),
            pl.BlockSpec((tm, nr), lambda i: (i, 0)),
        ],
        out_specs=[
            pl.BlockSpec((tm, rq), lambda i: (i, 0)),
            pl.BlockSpec((tm, rkv), lambda i: (i, 0)),
            pl.BlockSpec((tm, nr), lambda i: (i, 0)),
        ],
        out_shape=[
            jax.ShapeDtypeStruct((m, rq), BF16),
            jax.ShapeDtypeStruct((m, rkv), BF16),
            jax.ShapeDtypeStruct((m, nr), BF16),
        ],
        compiler_params=_params(("arbitrary",),
                                _vmem_limit(tm * k * 2 + k * n * 2 + tm * nr * 8 + tm * n * 2, 0, tm * n * 8)),
    )(h, w_cat, q_g.reshape(1, rq), kv_g.reshape(1, rkv), cos, sin)


def _mla_q_kernel(a_ref, w_ref, cos_ref, sin_ref, o_ref, *, hw, nope):
    acc = jnp.dot(a_ref[...], w_ref[...], preferred_element_type=F32)
    cos = cos_ref[...]
    sin = sin_ref[...]
    for hh in range(acc.shape[1] // hw):
        o_ref[:, hh * hw:hh * hw + nope] = acc[:, hh * hw:hh * hw + nope].astype(o_ref.dtype)
        o_ref[:, hh * hw + nope:(hh + 1) * hw] = _rope_pair(
            acc[:, hh * hw + nope:(hh + 1) * hw], cos, sin).astype(o_ref.dtype)


def _mla_q(cq, w_uq, cos, sin, hw, nope):
    m, k = cq.shape
    n = w_uq.shape[1]
    tm = _pick_tile(m, (1024, 512, 256))
    tn = _pick_tile(n, (1024, 512, 256))
    nr = hw - nope
    return pl.pallas_call(
        functools.partial(_mla_q_kernel, hw=hw, nope=nope),
        grid=(m // tm, n // tn),
        in_specs=[
            pl.BlockSpec((tm, k), lambda i, j: (i, 0)),
            pl.BlockSpec((k, tn), lambda i, j: (0, j)),
            pl.BlockSpec((tm, nr), lambda i, j: (i, 0)),
            pl.BlockSpec((tm, nr), lambda i, j: (i, 0)),
        ],
        out_specs=pl.BlockSpec((tm, tn), lambda i, j: (i, j)),
        out_shape=jax.ShapeDtypeStruct((m, n), BF16),
        compiler_params=_params(("arbitrary", "arbitrary"),
                                _vmem_limit(tm * k * 2 + k * tn * 2 + tm * nr * 8 + tm * tn * 2, 0, tm * tn * 8)),
    )(cq, w_uq, cos, sin)


def _attn_kernel(q_ref, kn_ref, kr_ref, v_ref, o_ref, kh_ref, *, ctx):
    t = pl.program_id(2)
    nope = kn_ref.shape[1]

    @pl.when(t == 0)
    def _():
        kh_ref[:, :nope] = kn_ref[...]
        kh_ref[:, nope:] = kr_ref[...]

    def attend(k, v):
        s = lax.dot_general(q_ref[...], k, (((1,), (1,)), ((), ())), preferred_element_type=F32)
        p = jnp.exp(s - jnp.max(s, axis=-1, keepdims=True))
        den = jnp.sum(p, axis=-1, keepdims=True)
        o = jnp.dot(p.astype(BF16), v, preferred_element_type=F32)
        return (o / den).astype(o_ref.dtype)

    @pl.when(t == 0)
    def _():
        o_ref[...] = attend(kh_ref[:ctx, :], v_ref[:ctx, :])

    @pl.when(t > 0)
    def _():
        o_ref[...] = attend(kh_ref[...], v_ref[...])


def _attention(q, kv, kr, batch, heads, ctx, hw, nope, dv):
    m = q.shape[0]
    s_len = m // batch
    nsb = s_len // ctx
    return pl.pallas_call(
        functools.partial(_attn_kernel, ctx=ctx),
        grid=(batch, heads, nsb),
        in_specs=[
            pl.BlockSpec((ctx, hw), lambda b, h, t: (b * nsb + t, h)),
            pl.BlockSpec((s_len, nope), lambda b, h, t: (b, h)),
            pl.BlockSpec((s_len, hw - nope), lambda b, h, t: (b, 0)),
            pl.BlockSpec((s_len, dv), lambda b, h, t: (b, heads * nope // dv + h)),
        ],
        out_specs=pl.BlockSpec((ctx, dv), lambda b, h, t: (b * nsb + t, h)),
        out_shape=jax.ShapeDtypeStruct((m, heads * dv), BF16),
        scratch_shapes=[pltpu.VMEM((s_len, hw), BF16)],
        compiler_params=_params(("arbitrary", "arbitrary", "arbitrary"),
                                _vmem_limit(ctx * hw * 2 + s_len * (hw + dv) * 2 + ctx * dv * 2,
                                            s_len * hw * 2, ctx * s_len * 16)),
    )(q, kv, kr, kv)


def _ret_qk_kernel(a_ref, w_ref, cos_ref, sin_ref, o_ref, *, dk, scale):
    acc = jnp.dot(a_ref[...], w_ref[...], preferred_element_type=F32)
    cos = cos_ref[...]
    sin = sin_ref[...]
    half = dk // 2
    for hh in range(acc.shape[1] // dk):
        x1 = acc[:, hh * dk:hh * dk + half]
        x2 = acc[:, hh * dk + half:(hh + 1) * dk]
        o_ref[:, hh * dk:hh * dk + half] = ((x1 * cos - x2 * sin) * scale).astype(o_ref.dtype)
        o_ref[:, hh * dk + half:(hh + 1) * dk] = ((x2 * cos + x1 * sin) * scale).astype(o_ref.dtype)


def _ret_qk(h, w, cos, sin, dk, scale, out_dtype):
    m, k = h.shape
    n = w.shape[1]
    tm = _pick_tile(m, (1024, 512, 256))
    tn = _pick_tile(n, (1024, 512, 256))
    half = dk // 2
    osz = jnp.dtype(out_dtype).itemsize
    return pl.pallas_call(
        functools.partial(_ret_qk_kernel, dk=dk, scale=scale),
        grid=(m // tm, n // tn),
        in_specs=[
            pl.BlockSpec((tm, k), lambda i, j: (i, 0)),
            pl.BlockSpec((k, tn), lambda i, j: (0, j)),
            pl.BlockSpec((tm, half), lambda i, j: (i, 0)),
            pl.BlockSpec((tm, half), lambda i, j: (i, 0)),
        ],
        out_specs=pl.BlockSpec((tm, tn), lambda i, j: (i, j)),
        out_shape=jax.ShapeDtypeStruct((m, n), out_dtype),
        compiler_params=_params(("arbitrary", "arbitrary"),
                                _vmem_limit(tm * k * 2 + k * tn * 2 + tm * half * 8 + tm * tn * osz, 0, tm * tn * 8)),
    )(h, w, cos, sin)


def _ret_chunk_kernel(cd_ref, q_ref, k_ref, v_ref, intra_ref, xi_ref, zeta_ref, y_ref, state_ref, *, heads):
    t = pl.program_id(1)

    @pl.when(t == 0)
    def _():
        state_ref[...] = jnp.zeros_like(state_ref)

    dk = q_ref.shape[1] // heads
    dv = v_ref.shape[1] // heads
    for h in range(heads):
        q = q_ref[:, h * dk:(h + 1) * dk]
        kf = k_ref[:, h * dk:(h + 1) * dk]
        v = v_ref[:, h * dv:(h + 1) * dv]
        st = state_ref[h]
        s = lax.dot_general(q, kf.astype(BF16), (((1,), (1,)), ((), ())),
                            preferred_element_type=F32) * intra_ref[h]
        y = jnp.dot(s.astype(BF16), v, preferred_element_type=F32)
        y = y + jnp.dot(q, st.astype(BF16), preferred_element_type=F32) * xi_ref[h]
        y_ref[:, h * dv:(h + 1) * dv] = y
        kz = (kf * zeta_ref[h]).astype(BF16)
        upd = lax.dot_general(kz, v, (((0,), (0,)), ((), ())), preferred_element_type=F32)
        state_ref[h] = st * cd_ref[h] + upd


def _ret_chunks(q, k, v, tables, batch, heads, ctx, backward):
    m = q.shape[0]
    c = RET_CHUNK
    nch = m // batch // c
    ncc = ctx // c
    cd, intra, xi, zeta = tables
    dkh = q.shape[1]
    dvh = v.shape[1]

    if backward:
        def chunk(t):
            return jnp.where(t < ncc, ncc - 1 - t, nch - 1 - (t - ncc))
    else:
        def chunk(t):
            return t

    row = lambda b, t: (b * nch + chunk(t), 0)
    const3 = lambda b, t: (0, 0, 0)
    return pl.pallas_call(
        functools.partial(_ret_chunk_kernel, heads=heads),
        grid=(batch, nch),
        in_specs=[
            pl.BlockSpec(memory_space=pltpu.SMEM),
            pl.BlockSpec((c, dkh), row),
            pl.BlockSpec((c, dkh), row),
            pl.BlockSpec((c, dvh), row),
            pl.BlockSpec(intra.shape, const3),
            pl.BlockSpec(xi.shape, const3),
            pl.BlockSpec(zeta.shape, const3),
        ],
        out_specs=pl.BlockSpec((c, dvh), row),
        out_shape=jax.ShapeDtypeStruct((m, dvh), F32),
        scratch_shapes=[pltpu.VMEM((heads, dkh // heads, dvh // heads), F32)],
        compiler_params=_params(("arbitrary", "arbitrary"),
                                _vmem_limit(c * dkh * 6 + c * dvh * 6 + (intra.size + xi.size + zeta.size) * 4,
                                            dkh * dvh // heads * 4, 8 << 20)),
    )(cd, q, k, v, intra, xi, zeta)


def _ret_tables(decay, dk, dv, backward):
    c = RET_CHUNK
    lg = -jnp.exp(decay.astype(F32))
    pos = jnp.arange(c, dtype=F32)
    diff = pos[:, None] - pos[None, :]
    lower = diff >= 0
    intra = jnp.where(lower, jnp.exp(lg[:, None, None] * jnp.where(lower, diff, 0.0)), 0.0)
    xi = jnp.exp(lg[:, None] * (pos + 1.0))
    zeta = jnp.exp(lg[:, None] * (c - 1.0 - pos))
    cd = jnp.exp(lg * c)
    if backward:
        intra = jnp.flip(intra, axis=(1, 2))
        xi = jnp.flip(xi, axis=1)
        zeta = jnp.flip(zeta, axis=1)
    heads = decay.shape[0]
    xi = jnp.broadcast_to(xi[:, :, None], (heads, c, dv))
    zeta = jnp.broadcast_to(zeta[:, :, None], (heads, c, dk))
    return cd, intra, xi, zeta


def _ret_mix_kernel(yf_ref, yb_ref, gf_ref, gb_ref, o_ref):
    def head_norm(y):
        d = y - jnp.mean(y, axis=-1, keepdims=True)
        return d * lax.rsqrt(jnp.mean(d * d, axis=-1, keepdims=True) + GN_EPS)

    o_ref[...] = (gf_ref[...] * head_norm(yf_ref[...]) + gb_ref[...] * head_norm(yb_ref[...])).astype(o_ref.dtype)


def _ret_mix(yf, yb, gf, gb, dv):
    m, n = yf.shape
    tm = _pick_tile(m, (512, 256))
    spec = pl.BlockSpec((tm, dv), lambda i, j: (i, j))
    return pl.pallas_call(
        _ret_mix_kernel,
        grid=(m // tm, n // dv),
        in_specs=[spec, spec, spec, spec],
        out_specs=spec,
        out_shape=jax.ShapeDtypeStruct((m, n), BF16),
        compiler_params=_params(("arbitrary", "arbitrary"), _vmem_limit(tm * dv * 18, 0, tm * dv * 16)),
    )(yf, yb, gf, gb)


def _ffn_gu_kernel(a_ref, ap_ref, an_ref, wg_ref, wu_ref, cw_ref, cb_ref, o_ref, aext_ref, *, s_len, sub):
    i = pl.program_id(0)
    tm = a_ref.shape[0]
    halo = ap_ref.shape[0]

    @pl.when(pl.program_id(1) == 0)
    def _():
        aext_ref[:halo, :] = ap_ref[...]
        aext_ref[halo:halo + tm, :] = a_ref[...]
        aext_ref[halo + tm:, :] = an_ref[...]

    g_ext = jnp.dot(aext_ref[...], wg_ref[...], preferred_element_type=F32)
    u = jnp.dot(a_ref[...], wu_ref[...], preferred_element_type=F32)
    rows = g_ext.shape[0]
    g_prev = pltpu.roll(g_ext, 1, 0)[halo:halo + tm]
    g_next = pltpu.roll(g_ext, rows - 1, 0)[halo:halo + tm]
    g_cur = g_ext[halo:halo + tm]

    r = (i * tm + lax.broadcasted_iota(jnp.int32, (tm, 1), 0)).astype(F32)
    rb = r - jnp.floor((r + 0.5) * (1.0 / s_len)) * s_len
    g_prev = jnp.where((rb == 0.0) | (rb == float(sub)), 0.0, g_prev)
    g_next = jnp.where((rb == float(sub - 1)) | (rb == float(s_len - 1)), 0.0, g_next)
    g = g_prev * cw_ref[0:1, :] + g_cur * cw_ref[1:2, :] + g_next * cw_ref[2:3, :] + cb_ref[...]
    o_ref[...] = (_silu(g) * u).astype(o_ref.dtype)


def _ffn_gate_up(h, w_gate, w_up, conv_w, conv_b, s_len, sub):
    m, k = h.shape
    n = w_gate.shape[1]
    tm = _pick_tile(m, (1024, 512, 256))
    tn = _pick_tile(n, (512, 256, 128))
    halo = BF16_SUBLANES
    nhb = m // halo
    return pl.pallas_call(
        functools.partial(_ffn_gu_kernel, s_len=s_len, sub=sub),
        grid=(m // tm, n // tn),
        in_specs=[
            pl.BlockSpec((tm, k), lambda i, j: (i, 0)),
            pl.BlockSpec((halo, k), lambda i, j: (jnp.maximum(i * (tm // halo) - 1, 0), 0)),
            pl.BlockSpec((halo, k), lambda i, j: (jnp.minimum((i + 1) * (tm // halo), nhb - 1), 0)),
            pl.BlockSpec((k, tn), lambda i, j: (0, j)),
            pl.BlockSpec((k, tn), lambda i, j: (0, j)),
            pl.BlockSpec((CONV_W, tn), lambda i, j: (0, j)),
            pl.BlockSpec((1, tn), lambda i, j: (0, j)),
        ],
        out_specs=pl.BlockSpec((tm, tn), lambda i, j: (i, j)),
        out_shape=jax.ShapeDtypeStruct((m, n), BF16),
        scratch_shapes=[pltpu.VMEM((tm + 2 * halo, k), BF16)],
        compiler_params=_params(("arbitrary", "arbitrary"),
                                _vmem_limit((tm + 2 * halo) * k * 2 + 2 * k * tn * 2 + tm * tn * 2,
                                            (tm + 2 * halo) * k * 2, tm * tn * 32)),
    )(h, h, h, w_gate, w_up, conv_w, conv_b.reshape(1, n))


def _expand_rows(tab_x, ctx_fill, batch, ctx):
    n = tab_x.shape[1]
    full = jnp.concatenate([jnp.broadcast_to(ctx_fill, (ctx, n)).astype(F32), tab_x], axis=0)
    return jnp.tile(full, (batch, 1))


def _mla_rope_tables(seq, batch, ctx):
    rows = seq // GRID_W
    row = jnp.repeat(jnp.arange(rows), GRID_W).astype(F32)
    col = jnp.tile(jnp.arange(GRID_W), rows).astype(F32)
    axis_dim = MLA_ROPE // 2
    m = axis_dim // 2
    inv_axis = ROPE_BASE ** (-jnp.arange(0, axis_dim, 2, dtype=F32) / axis_dim)
    cos_parts, sin_parts = [], []
    for ang in (row[:, None] * inv_axis[None], col[:, None] * inv_axis[None]):
        c, s = jnp.cos(ang), jnp.sin(ang)
        cos_parts += [c, c]
        sin_parts += [-s, s]
    zeros = jnp.zeros((seq, MLA_ROPE), F32)
    cos = jnp.concatenate(cos_parts + [zeros], axis=1)
    sin = jnp.concatenate(sin_parts + [zeros], axis=1)
    fill_cos = jnp.concatenate([jnp.ones((MLA_ROPE,), F32), jnp.zeros((MLA_ROPE,), F32)])
    cos = _expand_rows(cos, fill_cos, batch, ctx)
    sin = _expand_rows(sin, jnp.zeros((2 * MLA_ROPE,), F32), batch, ctx)
    partner = []
    for base in (0, axis_dim):
        partner += [base + m + d for d in range(m)] + [base + d for d in range(m)]
    return cos, sin, jnp.array(partner, jnp.int32)


def _ret_rope_tables(seq, dk, batch, ctx):
    inv_ret = ROPE_BASE ** (-jnp.arange(0, dk, 2, dtype=F32) / dk)
    ang = jnp.arange(seq, dtype=F32)[:, None] * inv_ret[None]
    cos = _expand_rows(jnp.cos(ang), jnp.ones((dk // 2,), F32), batch, ctx)
    sin = _expand_rows(jnp.sin(ang), jnp.zeros((dk // 2,), F32), batch, ctx)
    return cos, sin


def _sub_table(vec_x, vec_c, nsb):
    b, d = vec_x.shape
    full = jnp.concatenate([jnp.broadcast_to(vec_c[None, None, :], (b, 1, d)),
                            jnp.broadcast_to(vec_x[:, None, :], (b, nsb - 1, d))], axis=1)
    return full.reshape(b * nsb, 1, d)


def _mla_layer(h, xc, gate, tabs, batch, ctx, w_dq, q_g, w_uq, w_dkv, kv_g, w_ukv, w_o):
    cos, sin, partner = tabs
    rq = w_dq.shape[1]
    rkv = kv_g.shape[0]
    heads, nope, rope, dv = MLA_HEADS, MLA_NOPE, MLA_ROPE, MLA_V
    hw = nope + 2 * rope
    w_kr = w_dkv[:, rkv:]
    w_cat = jnp.concatenate([w_dq, w_dkv[:, :rkv], w_kr, w_kr[:, partner]], axis=1).astype(BF16)
    cq, ckv, kr = _mla_down(h, w_cat, q_g, kv_g, cos, sin, rq, rkv)

    scale = float(nope + rope) ** -0.5
    wq = w_uq.reshape(rq, heads, nope + rope) * scale
    wq = jnp.concatenate([wq, wq[:, :, nope:][:, :, partner]], axis=2).reshape(rq, heads * hw).astype(BF16)
    q = _mla_q(cq, wq, cos, sin, hw, nope)

    wkv = w_ukv.reshape(rkv, heads, nope + dv)
    wkv = jnp.concatenate([wkv[:, :, :nope].reshape(rkv, heads * nope),
                           wkv[:, :, nope:].reshape(rkv, heads * dv)], axis=1).astype(BF16)
    kv = _matmul(ckv, wkv, BF16)

    o = _attention(q, kv, kr, batch, heads, ctx, hw, nope, dv)
    return _matmul_gated_residual(o, w_o.astype(BF16), xc, gate, ctx)


def _ret_layer(h, xc, gate, tabs, batch, ctx, w_q, w_k, w_v, w_gf, w_gb, w_o, decay_f, decay_b):
    cos, sin = tabs
    heads = RET_HEADS
    dk = w_q.shape[1] // heads
    dv = w_v.shape[1] // heads
    q = _ret_qk(h, w_q.astype(BF16), cos, sin, dk, 1.0, BF16)
    k = _ret_qk(h, w_k.astype(BF16), cos, sin, dk, float(dk) ** -0.5, F32)
    v = _matmul(h, w_v.astype(BF16), BF16)
    gf = _matmul(h, w_gf.astype(BF16), F32, act="silu")
    gb = _matmul(h, w_gb.astype(BF16), F32, act="silu")
    yf = _ret_chunks(q, k, v, _ret_tables(decay_f, dk, dv, False), batch, heads, ctx, False)
    yb = _ret_chunks(q, k, v, _ret_tables(decay_b, dk, dv, True), batch, heads, ctx, True)
    mix = _ret_mix(yf, yb, gf, gb, dv)
    return _matmul_gated_residual(mix, w_o.astype(BF16), xc, gate, ctx)


def kernel(x, c, ctx, c_ctx, mod_w, mod_b, norm_mix_g, norm_ffn_g, mla_w_dq, mla_q_norm_g, mla_w_uq, mla_w_dkv, mla_kv_norm_g, mla_w_ukv, mla_w_o, ret_w_q, ret_w_k, ret_w_v, ret_w_gf, ret_w_gb, ret_w_o, ret_decay_f, ret_decay_b, ffn_w_gate, ffn_w_up, ffn_conv_w, ffn_conv_b, ffn_w_down, final_norm_g):
    batch, seq, d = x.shape
    ctx_len = ctx.shape[1]
    depth = mod_w.shape[0]
    s_len = ctx_len + seq
    nsb = s_len // ctx_len
    assert seq % ctx_len == 0 and ctx_len % RET_CHUNK == 0 and seq % GRID_W == 0
    assert mod_w.shape[2] == N_MOD * d

    xc = jnp.concatenate([ctx, x], axis=1).reshape(batch * s_len, d)

    cond_rows = 2 * 8
    cond = jnp.zeros((cond_rows, d), F32).at[:batch].set(c).at[batch].set(c_ctx)
    assert batch + 1 <= cond_rows
    mod = _modulation(cond, mod_w, mod_b)

    mla_tabs = _mla_rope_tables(seq, batch, ctx_len)
    ret_tabs = _ret_rope_tables(seq, ret_w_q.shape[2] // RET_HEADS, batch, ctx_len)

    for l in range(depth):
        i = l // N_MIXERS
        parts = [_sub_table(mod[l, :batch, k * d:(k + 1) * d], mod[l, batch, k * d:(k + 1) * d], nsb)
                 for k in range(N_MOD)]
        sh1, sc1, g1, sh2, sc2, g2 = parts
        h = _norm_mod(xc, norm_mix_g[l], sc1, sh1, ctx_len)
        if l % N_MIXERS == 0:
            xc = _mla_layer(h, xc, g1, mla_tabs, batch, ctx_len, mla_w_dq[i], mla_q_norm_g[i], mla_w_uq[i],
                            mla_w_dkv[i], mla_kv_norm_g[i], mla_w_ukv[i], mla_w_o[i])
        else:
            xc = _ret_layer(h, xc, g1, ret_tabs, batch, ctx_len, ret_w_q[i], ret_w_k[i], ret_w_v[i],
                            ret_w_gf[i], ret_w_gb[i], ret_w_o[i], ret_decay_f[i], ret_decay_b[i])
        h2 = _norm_mod(xc, norm_ffn_g[l], sc2, sh2, ctx_len)
        act = _ffn_gate_up(h2, ffn_w_gate[l].astype(BF16), ffn_w_up[l].astype(BF16),
                           ffn_conv_w[l], ffn_conv_b[l], s_len, ctx_len)
        xc = _matmul_gated_residual(act, ffn_w_down[l].astype(BF16), xc, g2, ctx_len)

    return _final_norm(xc, final_norm_g, batch, seq, ctx_len).reshape(batch, seq, d)
```

```python
import functools

import jax
import jax.numpy as jnp
from jax import lax
from jax.experimental import pallas as pl
from jax.experimental.pallas import tpu as pltpu

GRID_W = 64
N_MIXERS = 2
MLA_HEADS = 16
MLA_NOPE = 128
MLA_ROPE = 64
MLA_V = 128
RET_HEADS = 8
RET_CHUNK = 128
CONV_W = 3
N_MOD = 6
ROPE_BASE = 10000.0
NORM_EPS = 1e-6
GN_EPS = 1e-5

V7X_VMEM_BYTES = 64 * 1024 * 1024
LANES = 128
BF16_SUBLANES = 16
MXU_COL = 256
VREG_ELEMS = 8 * LANES
RED_VREGS = 16
ATTN_TQ = 512

F32 = jnp.float32
BF16 = jnp.bfloat16


def _vmem_limit(block_bytes, scratch_bytes=0, temp_bytes=0):
    need = 2 * block_bytes + scratch_bytes + temp_bytes + (4 << 20)
    return int(min(need, V7X_VMEM_BYTES - (6 << 20)))


def _params(sem, limit):
    return pltpu.CompilerParams(dimension_semantics=sem, vmem_limit_bytes=limit)


def _pick_tile(n, prefs):
    for t in prefs:
        if n % t == 0:
            return t
    raise ValueError(f"no tile in {prefs} divides {n}")


def _silu(x):
    return x * (1.0 / (1.0 + jnp.exp(-x)))


def _mod_kernel(a_ref, w_ref, b_ref, o_ref):
    a = _silu(a_ref[...]).astype(BF16)
    acc = jnp.dot(a, w_ref[0].astype(BF16), preferred_element_type=F32)
    o_ref[0] = acc + b_ref[0]


def _modulation(cond, mod_w, mod_b):
    depth, d, n = mod_w.shape
    rows = cond.shape[0]
    tn = _pick_tile(n, (1024, 512, 256, 128))
    return pl.pallas_call(
        _mod_kernel,
        grid=(depth, n // tn),
        in_specs=[
            pl.BlockSpec((rows, d), lambda l, j: (0, 0)),
            pl.BlockSpec((1, d, tn), lambda l, j: (l, 0, j)),
            pl.BlockSpec((1, 1, tn), lambda l, j: (l, 0, j)),
        ],
        out_specs=pl.BlockSpec((1, rows, tn), lambda l, j: (l, 0, j)),
        out_shape=jax.ShapeDtypeStruct((depth, rows, n), F32),
        compiler_params=_params(("arbitrary", "arbitrary"),
                                _vmem_limit(d * tn * 4 + rows * (d + tn) * 4, 0, d * tn * 2)),
    )(cond, mod_w, mod_b.reshape(depth, 1, n))


def _norm_mod_kernel(x_ref, g_ref, sc_ref, sh_ref, o_ref):
    x = x_ref[...]
    y = x * lax.rsqrt(jnp.mean(x * x, axis=-1, keepdims=True) + NORM_EPS) * g_ref[...]
    o_ref[...] = (y * (1.0 + sc_ref[0]) + sh_ref[0]).astype(o_ref.dtype)


def _norm_mod(xc, g, sc, sh, sub):
    m, d = xc.shape
    return pl.pallas_call(
        _norm_mod_kernel,
        grid=(m // sub,),
        in_specs=[
            pl.BlockSpec((sub, d), lambda i: (i, 0)),
            pl.BlockSpec((1, d), lambda i: (0, 0)),
            pl.BlockSpec((1, 1, d), lambda i: (i, 0, 0)),
            pl.BlockSpec((1, 1, d), lambda i: (i, 0, 0)),
        ],
        out_specs=pl.BlockSpec((sub, d), lambda i: (i, 0)),
        out_shape=jax.ShapeDtypeStruct((m, d), BF16),
        compiler_params=_params(("arbitrary",), _vmem_limit(sub * d * 6, 0, sub * d * 8)),
    )(xc, g.reshape(1, d), sc, sh)


def _final_norm_kernel(x_ref, g_ref, o_ref):
    x = x_ref[...]
    o_ref[...] = x * lax.rsqrt(jnp.mean(x * x, axis=-1, keepdims=True) + NORM_EPS) * g_ref[...]


def _final_norm(xc, g, batch, seq, sub):
    m, d = xc.shape
    nsb = m // batch // sub
    nx = seq // sub
    return pl.pallas_call(
        _final_norm_kernel,
        grid=(batch, nx),
        in_specs=[
            pl.BlockSpec((sub, d), lambda b, j: (b * nsb + j, 0)),
            pl.BlockSpec((1, d), lambda b, j: (0, 0)),
        ],
        out_specs=pl.BlockSpec((sub, d), lambda b, j: (b * nx + j, 0)),
        out_shape=jax.ShapeDtypeStruct((batch * seq, d), F32),
        compiler_params=_params(("arbitrary", "arbitrary"), _vmem_limit(sub * d * 8, 0, sub * d * 8)),
    )(xc, g.reshape(1, d))


def _mm_kernel(a_ref, w_ref, o_ref, *, act):
    acc = jnp.dot(a_ref[...], w_ref[...], preferred_element_type=F32)
    if act == "silu":
        acc = _silu(acc)
    o_ref[...] = acc.astype(o_ref.dtype)


def _matmul(a, w, out_dtype, act=None):
    m, k = a.shape
    n = w.shape[1]
    tm = _pick_tile(m, (1024, 512, 256))
    tn = _pick_tile(n, (1024, 512, 256, 128))
    osz = jnp.dtype(out_dtype).itemsize
    return pl.pallas_call(
        functools.partial(_mm_kernel, act=act),
        grid=(m // tm, n // tn),
        in_specs=[
            pl.BlockSpec((tm, k), lambda i, j: (i, 0)),
            pl.BlockSpec((k, tn), lambda i, j: (0, j)),
        ],
        out_specs=pl.BlockSpec((tm, tn), lambda i, j: (i, j)),
        out_shape=jax.ShapeDtypeStruct((m, n), out_dtype),
        compiler_params=_params(("arbitrary", "arbitrary"),
                                _vmem_limit(tm * k * 2 + k * tn * 2 + tm * tn * osz, 0, tm * tn * 8)),
    )(a, w)


def _mm_res_kernel(a_ref, w_ref, res_ref, gate_ref, *rest, sub):
    acc = jnp.dot(a_ref[...], w_ref[...], preferred_element_type=F32)
    nsub = acc.shape[0] // sub
    if len(rest) == 1:
        (x_ref,) = rest
        for s in range(nsub):
            rows = slice(s * sub, (s + 1) * sub)
            x_ref[rows, :] = res_ref[rows, :] + gate_ref[s] * acc[rows, :]
        return

    g_ref, sc_ref, sh_ref, x_ref, h_ref, row_ref = rest
    j = pl.program_id(1)
    nj = row_ref.shape[0]
    tn = row_ref.shape[2]
    for s in range(nsub):
        rows = slice(s * sub, (s + 1) * sub)
        xs = res_ref[rows, :] + gate_ref[s] * acc[rows, :]
        x_ref[rows, :] = xs
        row_ref[j, rows, :] = xs

    @pl.when(j == nj - 1)
    def _():
        ss = jnp.sum(row_ref[0] * row_ref[0], axis=-1, keepdims=True)
        for jj in range(1, nj):
            ss = ss + jnp.sum(row_ref[jj] * row_ref[jj], axis=-1, keepdims=True)
        inv = lax.rsqrt(ss * (1.0 / (nj * tn)) + NORM_EPS)
        for jj in range(nj):
            cols = slice(jj * tn, (jj + 1) * tn)
            for s in range(nsub):
                rows = slice(s * sub, (s + 1) * sub)
                y = row_ref[jj, rows, :] * inv[rows, :] * g_ref[:, cols]
                h_ref[rows, cols] = (y * (1.0 + sc_ref[s][:, cols]) + sh_ref[s][:, cols]).astype(h_ref.dtype)


def _matmul_gated_residual(a, w, res, gate, sub, norm=None):
    m, k = a.shape
    n = w.shape[1]
    tm = _pick_tile(m, (512, 256))
    tn = _pick_tile(n, (1024, 512, 256, 128) if k * 1024 * 2 <= (8 << 20) else (512, 256, 128))
    nsub = tm // sub
    in_specs = [
        pl.BlockSpec((tm, k), lambda i, j: (i, 0)),
        pl.BlockSpec((k, tn), lambda i, j: (0, j)),
        pl.BlockSpec((tm, tn), lambda i, j: (i, j)),
        pl.BlockSpec((nsub, 1, tn), lambda i, j: (i, 0, j)),
    ]
    x_spec = pl.BlockSpec((tm, tn), lambda i, j: (i, j))
    x_shape = jax.ShapeDtypeStruct((m, n), F32)
    blocks = tm * k * 2 + k * tn * 2 + tm * tn * 8
    if norm is None:
        return pl.pallas_call(
            functools.partial(_mm_res_kernel, sub=sub),
            grid=(m // tm, n // tn),
            in_specs=in_specs,
            out_specs=x_spec,
            out_shape=x_shape,
            compiler_params=_params(("arbitrary", "arbitrary"), _vmem_limit(blocks, 0, tm * tn * 8)),
        )(a, w, res, gate)
    g, sc, sh = norm
    return pl.pallas_call(
        functools.partial(_mm_res_kernel, sub=sub),
        grid=(m // tm, n // tn),
        in_specs=in_specs + [
            pl.BlockSpec((1, n), lambda i, j: (0, 0)),
            pl.BlockSpec((nsub, 1, n), lambda i, j: (i, 0, 0)),
            pl.BlockSpec((nsub, 1, n), lambda i, j: (i, 0, 0)),
        ],
        out_specs=[x_spec, pl.BlockSpec((tm, n), lambda i, j: (i, 0))],
        out_shape=[x_shape, jax.ShapeDtypeStruct((m, n), BF16)],
        scratch_shapes=[pltpu.VMEM((n // tn, tm, tn), F32)],
        compiler_params=_params(("arbitrary", "arbitrary"),
                                _vmem_limit(blocks + tm * n * 2, tm * n * 4, tm * tn * 8 + tm * n * 4)),
    )(a, w, res, gate, g.reshape(1, n), sc, sh)


def _rms(x, g):
    return x * lax.rsqrt(jnp.mean(x * x, axis=-1, keepdims=True) + NORM_EPS) * g


def _rope_pair(r, cos, sin):
    half = r.shape[-1] // 2
    return r * cos + pltpu.roll(r, half, 1) * sin


def _mla_down_kernel(a_ref, w_ref, qg_ref, kvg_ref, cos_ref, sin_ref, cq_ref, ckv_ref, kr_ref, *, rq, rkv):
    acc = jnp.dot(a_ref[...], w_ref[...], preferred_element_type=F32)
    cq_ref[...] = _rms(acc[:, :rq], qg_ref[...]).astype(cq_ref.dtype)
    ckv_ref[...] = _rms(acc[:, rq:rq + rkv], kvg_ref[...]).astype(ckv_ref.dtype)
    kr_ref[...] = _rope_pair(acc[:, rq + rkv:], cos_ref[...], sin_ref[...]).astype(kr_ref.dtype)


def _mla_down(h, w_cat, q_g, kv_g, cos, sin, rq, rkv):
    m, k = h.shape
    n = w_cat.shape[1]
    nr = n - rq - rkv
    tm = _pick_tile(m, (1024, 512, 256))
    return pl.pallas_call(
        functools.partial(_mla_down_kernel, rq=rq, rkv=rkv),
        grid=(m // tm,),
        in_specs=[
            pl.BlockSpec((tm, k), lambda i: (i, 0)),
            pl.BlockSpec((k, n), lambda i: (0, 0)),
            pl.BlockSpec((1, rq), lambda i: (0, 0)),
            pl.BlockSpec((1, rkv), lambda i: (0, 0)),
            pl.BlockSpec((tm, nr), lambda i: (i, 0)),
            pl.BlockSpec((tm, nr), lambda i: (i, 0)),
        ],
        out_specs=[
            pl.BlockSpec((tm, rq), lambda i: (i, 0)),
            pl.BlockSpec((tm, rkv), lambda i: (i, 0)),
            pl.BlockSpec((tm, nr), lambda i: (i, 0)),
        ],
        out_shape=[
            jax.ShapeDtypeStruct((m, rq), BF16),
            jax.ShapeDtypeStruct((m, rkv), BF16),
            jax.ShapeDtypeStruct((m, nr), BF16),
        ],
        compiler_params=_params(("arbitrary",),
                                _vmem_limit(tm * k * 2 + k * n * 2 + tm * nr * 8 + tm * n * 2, 0, tm * n * 8)),
    )(h, w_cat, q_g.reshape(1, rq), kv_g.reshape(1, rkv), cos, sin)


def _mla_q_kernel(a_ref, w_ref, cos_ref, sin_ref, o_ref, *, hw, nope):
    acc = jnp.dot(a_ref[...], w_ref[...], preferred_element_type=F32)
    cos = cos_ref[...]
    sin = sin_ref[...]
    for hh in range(acc.shape[1] // hw):
        o_ref[:, hh * hw:hh * hw + nope] = acc[:, hh * hw:hh * hw + nope].astype(o_ref.dtype)
        o_ref[:, hh * hw + nope:(hh + 1) * hw] = _rope_pair(
            acc[:, hh * hw + nope:(hh + 1) * hw], cos, sin).astype(o_ref.dtype)


def _mla_q(cq, w_uq, cos, sin, hw, nope):
    m, k = cq.shape
    n = w_uq.shape[1]
    tm = _pick_tile(m, (1024, 512, 256))
    tn = _pick_tile(n, (1024, 512, 256))
    nr = hw - nope
    return pl.pallas_call(
        functools.partial(_mla_q_kernel, hw=hw, nope=nope),
        grid=(m // tm, n // tn),
        in_specs=[
            pl.BlockSpec((tm, k), lambda i, j: (i, 0)),
            pl.BlockSpec((k, tn), lambda i, j: (0, j)),
            pl.BlockSpec((tm, nr), lambda i, j: (i, 0)),
            pl.BlockSpec((tm, nr), lambda i, j: (i, 0)),
        ],
        out_specs=pl.BlockSpec((tm, tn), lambda i, j: (i, j)),
        out_shape=jax.ShapeDtypeStruct((m, n), BF16),
        compiler_params=_params(("arbitrary", "arbitrary"),
                                _vmem_limit(tm * k * 2 + k * tn * 2 + tm * nr * 8 + tm * tn * 2, 0, tm * tn * 8)),
    )(cq, w_uq, cos, sin)


def _attend_t(q, k, vt):
    nq, nk = q.shape[0], k.shape[0]
    st = lax.dot_general(k, q, (((1,), (1,)), ((), ())), preferred_element_type=F32)
    rows = RED_VREGS * VREG_ELEMS // nq
    st = st.reshape(nk // rows, rows, nq)
    p = jnp.exp2(st - jnp.max(jnp.max(st, axis=0), axis=0, keepdims=True))
    den = jnp.sum(jnp.sum(p, axis=0), axis=0, keepdims=True)
    ot = jnp.dot(vt, p.reshape(nk, nq).astype(BF16), preferred_element_type=F32)
    return (ot / den).T


def _attn_kernel(q_ref, kn_ref, kr_ref, v_ref, o_ref, kh_ref, vt_ref, *, ctx, n_x_tiles):
    t = pl.program_id(2)
    nope = kn_ref.shape[1]
    s_len = kh_ref.shape[0]

    @pl.when(t == 0)
    def _():
        kh_ref[:, :nope] = kn_ref[...]
        kh_ref[:, nope:] = kr_ref[...]
        vt_ref[...] = v_ref[...].astype(F32).T.astype(BF16)

    @pl.when(t < n_x_tiles)
    def _():
        o_ref[...] = _attend_t(q_ref[...], kh_ref[...], vt_ref[...]).astype(o_ref.dtype)

    @pl.when(t == n_x_tiles)
    def _():
        o_ref[:ctx, :] = _attend_t(q_ref[:ctx, :], kh_ref[s_len - ctx:, :],
                                   vt_ref[:, s_len - ctx:]).astype(o_ref.dtype)
        o_ref[ctx:, :] = jnp.zeros((o_ref.shape[0] - ctx, o_ref.shape[1]), o_ref.dtype)


def _attention(q, kv, kr, batch, heads, ctx, hw, nope, dv):
    m = q.shape[0]
    s_len = m // batch
    seq = s_len - ctx
    tq = ATTN_TQ
    assert seq % tq == 0 and ctx < tq
    n_x_tiles = seq // tq
    v_col0 = heads * nope // dv
    per_batch = lambda a: a.reshape(batch, s_len, a.shape[1])
    out = pl.pallas_call(
        functools.partial(_attn_kernel, ctx=ctx, n_x_tiles=n_x_tiles),
        grid=(batch, heads, n_x_tiles + 1),
        in_specs=[
            pl.BlockSpec((None, tq, hw), lambda b, h, t: (b, t, h)),
            pl.BlockSpec((None, s_len, nope), lambda b, h, t: (b, 0, h)),
            pl.BlockSpec((None, s_len, hw - nope), lambda b, h, t: (b, 0, 0)),
            pl.BlockSpec((None, s_len, dv), lambda b, h, t: (b, 0, v_col0 + h)),
        ],
        out_specs=pl.BlockSpec((None, tq, dv), lambda b, h, t: (b, t, h)),
        out_shape=jax.ShapeDtypeStruct((batch, s_len, heads * dv), BF16),
        scratch_shapes=[pltpu.VMEM((s_len, hw), BF16), pltpu.VMEM((dv, s_len), BF16)],
        compiler_params=_params(("arbitrary", "arbitrary", "arbitrary"),
                                _vmem_limit(tq * hw * 2 + s_len * (hw + dv) * 2 + tq * dv * 2,
                                            s_len * (hw + dv) * 2, tq * s_len * 10)),
    )(per_batch(q), per_batch(kv), per_batch(kr), per_batch(kv))
    return out.reshape(m, heads * dv)


def _ret_qk_kernel(a_ref, w_ref, cos_ref, sin_ref, o_ref, *, dk, scale):
    acc = jnp.dot(a_ref[...], w_ref[...], preferred_element_type=F32)
    cos = cos_ref[...]
    sin = sin_ref[...]
    half = dk // 2
    for hh in range(acc.shape[1] // dk):
        x1 = acc[:, hh * dk:hh * dk + half]
        x2 = acc[:, hh * dk + half:(hh + 1) * dk]
        o_ref[:, hh * dk:hh * dk + half] = ((x1 * cos - x2 * sin) * scale).astype(o_ref.dtype)
        o_ref[:, hh * dk + half:(hh + 1) * dk] = ((x2 * cos + x1 * sin) * scale).astype(o_ref.dtype)


def _ret_qk(h, w, cos, sin, dk, scale, out_dtype):
    m, k = h.shape
    n = w.shape[1]
    tm = _pick_tile(m, (1024, 512, 256))
    tn = _pick_tile(n, (1024, 512, 256))
    half = dk // 2
    osz = jnp.dtype(out_dtype).itemsize
    return pl.pallas_call(
        functools.partial(_ret_qk_kernel, dk=dk, scale=scale),
        grid=(m // tm, n // tn),
        in_specs=[
            pl.BlockSpec((tm, k), lambda i, j: (i, 0)),
            pl.BlockSpec((k, tn), lambda i, j: (0, j)),
            pl.BlockSpec((tm, half), lambda i, j: (i, 0)),
            pl.BlockSpec((tm, half), lambda i, j: (i, 0)),
        ],
        out_specs=pl.BlockSpec((tm, tn), lambda i, j: (i, j)),
        out_shape=jax.ShapeDtypeStruct((m, n), out_dtype),
        compiler_params=_params(("arbitrary", "arbitrary"),
                                _vmem_limit(tm * k * 2 + k * tn * 2 + tm * half * 8 + tm * tn * osz, 0, tm * tn * 8)),
    )(h, w, cos, sin)


def _head_norm(y):
    d = y - jnp.mean(y, axis=-1, keepdims=True)
    return d * lax.rsqrt(jnp.mean(d * d, axis=-1, keepdims=True) + GN_EPS)


def _ret_chunk_kernel(cd_ref, q_ref, k_ref, v_ref, intra_ref, xi_ref, zeta_ref, g_ref, *rest, heads):
    if len(rest) == 3:
        prev_ref, y_ref, state_ref = rest
    else:
        prev_ref = None
        y_ref, state_ref = rest
    t = pl.program_id(1)

    @pl.when(t == 0)
    def _():
        state_ref[...] = jnp.zeros_like(state_ref)

    dk = q_ref.shape[1] // heads
    dv = v_ref.shape[1] // heads
    for h in range(heads):
        q = q_ref[:, h * dk:(h + 1) * dk]
        kf = k_ref[:, h * dk:(h + 1) * dk]
        v = v_ref[:, h * dv:(h + 1) * dv]
        st = state_ref[h]
        s = lax.dot_general(q, kf.astype(BF16), (((1,), (1,)), ((), ())),
                            preferred_element_type=F32) * intra_ref[h]
        y = jnp.dot(s.astype(BF16), v, preferred_element_type=F32)
        y = y + jnp.dot(q, st.astype(BF16), preferred_element_type=F32) * xi_ref[h]
        cols = slice(h * dv, (h + 1) * dv)
        out = g_ref[:, cols] * _head_norm(y)
        if prev_ref is not None:
            out = prev_ref[:, cols] + out
        y_ref[:, cols] = out.astype(y_ref.dtype)
        kz = (kf * zeta_ref[h]).astype(BF16)
        upd = lax.dot_general(kz, v, (((0,), (0,)), ((), ())), preferred_element_type=F32)
        state_ref[h] = st * cd_ref[h] + upd


def _ret_chunks(q, k, v, gate, prev, tables, batch, heads, ctx, backward):
    m = q.shape[0]
    c = RET_CHUNK
    nch = m // batch // c
    ncc = ctx // c
    cd, intra, xi, zeta = tables
    dkh = q.shape[1]
    dvh = v.shape[1]

    nxc = nch - ncc
    if backward:
        def chunk(t):
            return nch - 1 - t
    else:
        def chunk(t):
            return jnp.where(t < ncc, nxc + t, t - ncc)

    row = lambda b, t: (b * nch + chunk(t), 0)
    const3 = lambda b, t: (0, 0, 0)
    extra = [gate] if prev is None else [gate, prev]
    return pl.pallas_call(
        functools.partial(_ret_chunk_kernel, heads=heads),
        grid=(batch, nch),
        in_specs=[
            pl.BlockSpec(memory_space=pltpu.SMEM),
            pl.BlockSpec((c, dkh), row),
            pl.BlockSpec((c, dkh), row),
            pl.BlockSpec((c, dvh), row),
            pl.BlockSpec(intra.shape, const3),
            pl.BlockSpec(xi.shape, const3),
            pl.BlockSpec(zeta.shape, const3),
        ] + [pl.BlockSpec((c, dvh), row)] * len(extra),
        out_specs=pl.BlockSpec((c, dvh), row),
        out_shape=jax.ShapeDtypeStruct((m, dvh), F32 if prev is None else BF16),
        scratch_shapes=[pltpu.VMEM((heads, dkh // heads, dvh // heads), F32)],
        compiler_params=_params(("arbitrary", "arbitrary"),
                                _vmem_limit(c * dkh * 6 + c * dvh * 14 + (intra.size + xi.size + zeta.size) * 4,
                                            dkh * dvh // heads * 4, 8 << 20)),
    )(cd, q, k, v, intra, xi, zeta, *extra)


def _ret_tables(decay, dk, dv, backward):
    c = RET_CHUNK
    lg = -jnp.exp(decay.astype(F32))
    pos = jnp.arange(c, dtype=F32)
    diff = pos[:, None] - pos[None, :]
    lower = diff >= 0
    intra = jnp.where(lower, jnp.exp(lg[:, None, None] * jnp.where(lower, diff, 0.0)), 0.0)
    xi = jnp.exp(lg[:, None] * (pos + 1.0))
    zeta = jnp.exp(lg[:, None] * (c - 1.0 - pos))
    cd = jnp.exp(lg * c)
    if backward:
        intra = jnp.flip(intra, axis=(1, 2))
        xi = jnp.flip(xi, axis=1)
        zeta = jnp.flip(zeta, axis=1)
    heads = decay.shape[0]
    xi = jnp.broadcast_to(xi[:, :, None], (heads, c, dv))
    zeta = jnp.broadcast_to(zeta[:, :, None], (heads, c, dk))
    return cd, intra, xi, zeta


def _ffn_gu_kernel(a_ref, ap_ref, an_ref, wg_ref, wu_ref, cw_ref, cb_ref, o_ref, aext_ref, *, s_len, sub):
    i = pl.program_id(0)
    tm = a_ref.shape[0]
    halo = ap_ref.shape[0]

    @pl.when(pl.program_id(1) == 0)
    def _():
        aext_ref[:halo, :] = ap_ref[...]
        aext_ref[halo:halo + tm, :] = a_ref[...]
        aext_ref[halo + tm:, :] = an_ref[...]

    g_ext = jnp.dot(aext_ref[...], wg_ref[...], preferred_element_type=F32)
    u = jnp.dot(a_ref[...], wu_ref[...], preferred_element_type=F32)
    rows = g_ext.shape[0]
    g_prev = pltpu.roll(g_ext, 1, 0)[halo:halo + tm]
    g_next = pltpu.roll(g_ext, rows - 1, 0)[halo:halo + tm]
    g_cur = g_ext[halo:halo + tm]

    r = (i * tm + lax.broadcasted_iota(jnp.int32, (tm, 1), 0)).astype(F32)
    rb = r - jnp.floor((r + 0.5) * (1.0 / s_len)) * s_len
    seq = s_len - sub
    g_prev = jnp.where((rb == 0.0) | (rb == float(seq)), 0.0, g_prev)
    g_next = jnp.where((rb == float(seq - 1)) | (rb == float(s_len - 1)), 0.0, g_next)
    g = g_prev * cw_ref[0:1, :] + g_cur * cw_ref[1:2, :] + g_next * cw_ref[2:3, :] + cb_ref[...]
    o_ref[...] = (_silu(g) * u).astype(o_ref.dtype)


def _ffn_gate_up(h, w_gate, w_up, conv_w, conv_b, s_len, sub):
    m, k = h.shape
    n = w_gate.shape[1]
    tm = _pick_tile(m, (1024, 512, 256))
    tn = _pick_tile(n, (512, 256, 128))
    halo = BF16_SUBLANES
    nhb = m // halo
    return pl.pallas_call(
        functools.partial(_ffn_gu_kernel, s_len=s_len, sub=sub),
        grid=(m // tm, n // tn),
        in_specs=[
            pl.BlockSpec((tm, k), lambda i, j: (i, 0)),
            pl.BlockSpec((halo, k), lambda i, j: (jnp.maximum(i * (tm // halo) - 1, 0), 0)),
            pl.BlockSpec((halo, k), lambda i, j: (jnp.minimum((i + 1) * (tm // halo), nhb - 1), 0)),
            pl.BlockSpec((k, tn), lambda i, j: (0, j)),
            pl.BlockSpec((k, tn), lambda i, j: (0, j)),
            pl.BlockSpec((CONV_W, tn), lambda i, j: (0, j)),
            pl.BlockSpec((1, tn), lambda i, j: (0, j)),
        ],
        out_specs=pl.BlockSpec((tm, tn), lambda i, j: (i, j)),
        out_shape=jax.ShapeDtypeStruct((m, n), BF16),
        scratch_shapes=[pltpu.VMEM((tm + 2 * halo, k), BF16)],
        compiler_params=_params(("arbitrary", "arbitrary"),
                                _vmem_limit((tm + 2 * halo) * k * 2 + 2 * k * tn * 2 + tm * tn * 2,
                                            (tm + 2 * halo) * k * 2, tm * tn * 32)),
    )(h, h, h, w_gate, w_up, conv_w, conv_b.reshape(1, n))


def _expand_rows(tab_x, ctx_fill, batch, ctx):
    n = tab_x.shape[1]
    full = jnp.concatenate([tab_x, jnp.broadcast_to(ctx_fill, (ctx, n)).astype(F32)], axis=0)
    return jnp.tile(full, (batch, 1))


def _mla_rope_tables(seq, batch, ctx):
    rows = seq // GRID_W
    row = jnp.repeat(jnp.arange(rows), GRID_W).astype(F32)
    col = jnp.tile(jnp.arange(GRID_W), rows).astype(F32)
    axis_dim = MLA_ROPE // 2
    m = axis_dim // 2
    inv_axis = ROPE_BASE ** (-jnp.arange(0, axis_dim, 2, dtype=F32) / axis_dim)
    cos_parts, sin_parts = [], []
    for ang in (row[:, None] * inv_axis[None], col[:, None] * inv_axis[None]):
        c, s = jnp.cos(ang), jnp.sin(ang)
        cos_parts += [c, c]
        sin_parts += [-s, s]
    zeros = jnp.zeros((seq, MLA_ROPE), F32)
    cos = jnp.concatenate(cos_parts + [zeros], axis=1)
    sin = jnp.concatenate(sin_parts + [zeros], axis=1)
    fill_cos = jnp.concatenate([jnp.ones((MLA_ROPE,), F32), jnp.zeros((MLA_ROPE,), F32)])
    cos = _expand_rows(cos, fill_cos, batch, ctx)
    sin = _expand_rows(sin, jnp.zeros((2 * MLA_ROPE,), F32), batch, ctx)
    partner = []
    for base in (0, axis_dim):
        partner += [base + m + d for d in range(m)] + [base + d for d in range(m)]
    return cos, sin, jnp.array(partner, jnp.int32)


def _ret_rope_tables(seq, dk, batch, ctx):
    inv_ret = ROPE_BASE ** (-jnp.arange(0, dk, 2, dtype=F32) / dk)
    ang = jnp.arange(seq, dtype=F32)[:, None] * inv_ret[None]
    cos = _expand_rows(jnp.cos(ang), jnp.ones((dk // 2,), F32), batch, ctx)
    sin = _expand_rows(jnp.sin(ang), jnp.zeros((dk // 2,), F32), batch, ctx)
    return cos, sin


def _sub_table(vec_x, vec_c, nsb):
    b, d = vec_x.shape
    full = jnp.concatenate([jnp.broadcast_to(vec_x[:, None, :], (b, nsb - 1, d)),
                            jnp.broadcast_to(vec_c[None, None, :], (b, 1, d))], axis=1)
    return full.reshape(b * nsb, 1, d)


def _mla_layer(h, xc, gate, norm, tabs, batch, ctx, w_dq, q_g, w_uq, w_dkv, kv_g, w_ukv, w_o):
    cos, sin, partner = tabs
    rq = w_dq.shape[1]
    rkv = kv_g.shape[0]
    heads, nope, rope, dv = MLA_HEADS, MLA_NOPE, MLA_ROPE, MLA_V
    hw = nope + 2 * rope
    w_kr = w_dkv[:, rkv:]
    w_cat = jnp.concatenate([w_dq, w_dkv[:, :rkv], w_kr, w_kr[:, partner]], axis=1).astype(BF16)
    cq, ckv, kr = _mla_down(h, w_cat, q_g, kv_g, cos, sin, rq, rkv)

    scale = float(nope + rope) ** -0.5 * 1.4426950408889634
    wq = w_uq.reshape(rq, heads, nope + rope) * scale
    wq = jnp.concatenate([wq, wq[:, :, nope:][:, :, partner]], axis=2).reshape(rq, heads * hw).astype(BF16)
    q = _mla_q(cq, wq, cos, sin, hw, nope)

    wkv = w_ukv.reshape(rkv, heads, nope + dv)
    wkv = jnp.concatenate([wkv[:, :, :nope].reshape(rkv, heads * nope),
                           wkv[:, :, nope:].reshape(rkv, heads * dv)], axis=1).astype(BF16)
    kv = _matmul(ckv, wkv, BF16)

    o = _attention(q, kv, kr, batch, heads, ctx, hw, nope, dv)
    return _matmul_gated_residual(o, w_o.astype(BF16), xc, gate, ctx, norm)


def _ret_layer(h, xc, gate, norm, tabs, batch, ctx, w_q, w_k, w_v, w_gf, w_gb, w_o, decay_f, decay_b):
    cos, sin = tabs
    heads = RET_HEADS
    dk = w_q.shape[1] // heads
    dv = w_v.shape[1] // heads
    q = _ret_qk(h, w_q.astype(BF16), cos, sin, dk, 1.0, BF16)
    k = _ret_qk(h, w_k.astype(BF16), cos, sin, dk, float(dk) ** -0.5, F32)
    v = _matmul(h, w_v.astype(BF16), BF16)
    gf = _matmul(h, w_gf.astype(BF16), F32, act="silu")
    gb = _matmul(h, w_gb.astype(BF16), F32, act="silu")
    fwd = _ret_chunks(q, k, v, gf, None, _ret_tables(decay_f, dk, dv, False), batch, heads, ctx, False)
    mix = _ret_chunks(q, k, v, gb, fwd, _ret_tables(decay_b, dk, dv, True), batch, heads, ctx, True)
    return _matmul_gated_residual(mix, w_o.astype(BF16), xc, gate, ctx, norm)


def kernel(x, c, ctx, c_ctx, mod_w, mod_b, norm_mix_g, norm_ffn_g, mla_w_dq, mla_q_norm_g, mla_w_uq, mla_w_dkv, mla_kv_norm_g, mla_w_ukv, mla_w_o, ret_w_q, ret_w_k, ret_w_v, ret_w_gf, ret_w_gb, ret_w_o, ret_decay_f, ret_decay_b, ffn_w_gate, ffn_w_up, ffn_conv_w, ffn_conv_b, ffn_w_down, final_norm_g):
    batch, seq, d = x.shape
    ctx_len = ctx.shape[1]
    depth = mod_w.shape[0]
    s_len = ctx_len + seq
    nsb = s_len // ctx_len
    assert seq % ctx_len == 0 and ctx_len % RET_CHUNK == 0 and seq % GRID_W == 0
    assert mod_w.shape[2] == N_MOD * d

    xc = jnp.concatenate([x, ctx], axis=1).reshape(batch * s_len, d)

    cond_rows = 2 * 8
    cond = jnp.zeros((cond_rows, d), F32).at[:batch].set(c).at[batch].set(c_ctx)
    assert batch + 1 <= cond_rows
    mod = _modulation(cond, mod_w, mod_b)

    mla_tabs = _mla_rope_tables(seq, batch, ctx_len)
    ret_tabs = _ret_rope_tables(seq, ret_w_q.shape[2] // RET_HEADS, batch, ctx_len)

    parts = [[_sub_table(mod[l, :batch, k * d:(k + 1) * d], mod[l, batch, k * d:(k + 1) * d], nsb)
              for k in range(N_MOD)] for l in range(depth)]

    h = _norm_mod(xc, norm_mix_g[0], parts[0][1], parts[0][0], ctx_len)
    for l in range(depth):
        i = l // N_MIXERS
        _, _, g1, sh2, sc2, g2 = parts[l]
        ffn_norm = (norm_ffn_g[l], sc2, sh2)
        if l % N_MIXERS == 0:
            xc, h2 = _mla_layer(h, xc, g1, ffn_norm, mla_tabs, batch, ctx_len, mla_w_dq[i], mla_q_norm_g[i],
                                mla_w_uq[i], mla_w_dkv[i], mla_kv_norm_g[i], mla_w_ukv[i], mla_w_o[i])
        else:
            xc, h2 = _ret_layer(h, xc, g1, ffn_norm, ret_tabs, batch, ctx_len, ret_w_q[i], ret_w_k[i], ret_w_v[i],
                                ret_w_gf[i], ret_w_gb[i], ret_w_o[i], ret_decay_f[i], ret_decay_b[i])
        act = _ffn_gate_up(h2, ffn_w_gate[l].astype(BF16), ffn_w_up[l].astype(BF16),
                           ffn_conv_w[l], ffn_conv_b[l], s_len, ctx_len)
        w_down = ffn_w_down[l].astype(BF16)
        if l + 1 < depth:
            xc, h = _matmul_gated_residual(act, w_down, xc, g2, ctx_len,
                                           (norm_mix_g[l + 1], parts[l + 1][1], parts[l + 1][0]))
        else:
            xc = _matmul_gated_residual(act, w_down, xc, g2, ctx_len)

    return _final_norm(xc, final_norm_g, batch, seq, ctx_len).reshape(batch, seq, d)
```

```python
import functools

import jax
import jax.numpy as jnp
from jax import lax
from jax.experimental import pallas as pl
from jax.experimental.pallas import tpu as pltpu

GRID_W = 64
N_MIXERS = 2
MLA_HEADS = 16
MLA_NOPE = 128
MLA_ROPE = 64
MLA_V = 128
RET_HEADS = 8
RET_CHUNK = 128
CONV_W = 3
N_MOD = 6
ROPE_BASE = 10000.0
NORM_EPS = 1e-6
GN_EPS = 1e-5

V7X_VMEM_BYTES = 64 * 1024 * 1024
LANES = 128
BF16_SUBLANES = 16
MXU_COL = 256
VREG_ELEMS = 8 * LANES
RED_VREGS = 16
ATTN_TQ = 512
SCAN_CHUNK = MXU_COL

F32 = jnp.float32
BF16 = jnp.bfloat16


def _vmem_limit(block_bytes, scratch_bytes=0, temp_bytes=0):
    need = 2 * block_bytes + scratch_bytes + temp_bytes + (4 << 20)
    return int(min(need, V7X_VMEM_BYTES - (6 << 20)))


def _params(sem, limit, flags=None):
    return pltpu.CompilerParams(dimension_semantics=sem, vmem_limit_bytes=limit, flags=flags)


def _pick_tile(n, prefs):
    for t in prefs:
        if n % t == 0:
            return t
    raise ValueError(f"no tile in {prefs} divides {n}")


def _silu(x):
    return x * (1.0 / (1.0 + jnp.exp(-x)))


def _mod_kernel(a_ref, w_ref, b_ref, o_ref):
    a = _silu(a_ref[...]).astype(BF16)
    acc = jnp.dot(a, w_ref[0].astype(BF16), preferred_element_type=F32)
    o_ref[0] = acc + b_ref[0]


def _modulation(cond, mod_w, mod_b):
    depth, d, n = mod_w.shape
    rows = cond.shape[0]
    tn = _pick_tile(n, (1024, 512, 256, 128))
    return pl.pallas_call(
        _mod_kernel,
        grid=(depth, n // tn),
        in_specs=[
            pl.BlockSpec((rows, d), lambda l, j: (0, 0)),
            pl.BlockSpec((1, d, tn), lambda l, j: (l, 0, j)),
            pl.BlockSpec((1, 1, tn), lambda l, j: (l, 0, j)),
        ],
        out_specs=pl.BlockSpec((1, rows, tn), lambda l, j: (l, 0, j)),
        out_shape=jax.ShapeDtypeStruct((depth, rows, n), F32),
        compiler_params=_params(("arbitrary", "arbitrary"),
                                _vmem_limit(d * tn * 4 + rows * (d + tn) * 4, 0, d * tn * 2)),
    )(cond, mod_w, mod_b.reshape(depth, 1, n))


def _norm_mod_kernel(x_ref, g_ref, sc_ref, sh_ref, o_ref):
    x = x_ref[...]
    y = x * lax.rsqrt(jnp.mean(x * x, axis=-1, keepdims=True) + NORM_EPS) * g_ref[...]
    o_ref[...] = (y * (1.0 + sc_ref[0]) + sh_ref[0]).astype(o_ref.dtype)


def _norm_mod(xc, g, sc, sh, sub):
    m, d = xc.shape
    return pl.pallas_call(
        _norm_mod_kernel,
        grid=(m // sub,),
        in_specs=[
            pl.BlockSpec((sub, d), lambda i: (i, 0)),
            pl.BlockSpec((1, d), lambda i: (0, 0)),
            pl.BlockSpec((1, 1, d), lambda i: (i, 0, 0)),
            pl.BlockSpec((1, 1, d), lambda i: (i, 0, 0)),
        ],
        out_specs=pl.BlockSpec((sub, d), lambda i: (i, 0)),
        out_shape=jax.ShapeDtypeStruct((m, d), BF16),
        compiler_params=_params(("arbitrary",), _vmem_limit(sub * d * 6, 0, sub * d * 8)),
    )(xc, g.reshape(1, d), sc, sh)


def _final_norm_kernel(x_ref, g_ref, o_ref):
    x = x_ref[...]
    o_ref[...] = x * lax.rsqrt(jnp.mean(x * x, axis=-1, keepdims=True) + NORM_EPS) * g_ref[...]


def _final_norm(xc, g, batch, seq, sub):
    m, d = xc.shape
    nsb = m // batch // sub
    nx = seq // sub
    return pl.pallas_call(
        _final_norm_kernel,
        grid=(batch, nx),
        in_specs=[
            pl.BlockSpec((sub, d), lambda b, j: (b * nsb + j, 0)),
            pl.BlockSpec((1, d), lambda b, j: (0, 0)),
        ],
        out_specs=pl.BlockSpec((sub, d), lambda b, j: (b * nx + j, 0)),
        out_shape=jax.ShapeDtypeStruct((batch * seq, d), F32),
        compiler_params=_params(("arbitrary", "arbitrary"), _vmem_limit(sub * d * 8, 0, sub * d * 8)),
    )(xc, g.reshape(1, d))


def _mm_kernel(a_ref, w_ref, o_ref, *, act):
    acc = jnp.dot(a_ref[...], w_ref[...], preferred_element_type=F32)
    if act == "silu":
        acc = _silu(acc)
    o_ref[...] = acc.astype(o_ref.dtype)


def _matmul(a, w, out_dtype, act=None):
    m, k = a.shape
    n = w.shape[1]
    tm = _pick_tile(m, (1024, 512, 256))
    tn = _pick_tile(n, (1024, 512, 256, 128))
    osz = jnp.dtype(out_dtype).itemsize
    return pl.pallas_call(
        functools.partial(_mm_kernel, act=act),
        grid=(m // tm, n // tn),
        in_specs=[
            pl.BlockSpec((tm, k), lambda i, j: (i, 0)),
            pl.BlockSpec((k, tn), lambda i, j: (0, j)),
        ],
        out_specs=pl.BlockSpec((tm, tn), lambda i, j: (i, j)),
        out_shape=jax.ShapeDtypeStruct((m, n), out_dtype),
        compiler_params=_params(("arbitrary", "arbitrary"),
                                _vmem_limit(tm * k * 2 + k * tn * 2 + tm * tn * osz, 0, tm * tn * 8)),
    )(a, w)


def _mm_res_kernel(a_ref, w_ref, res_ref, gate_ref, *rest, sub):
    acc = jnp.dot(a_ref[...], w_ref[...], preferred_element_type=F32)
    nsub = acc.shape[0] // sub
    if len(rest) == 1:
        (x_ref,) = rest
        for s in range(nsub):
            rows = slice(s * sub, (s + 1) * sub)
            x_ref[rows, :] = res_ref[rows, :] + gate_ref[s] * acc[rows, :]
        return

    g_ref, sc_ref, sh_ref, x_ref, h_ref, row_ref = rest
    j = pl.program_id(1)
    nj = row_ref.shape[0]
    tn = row_ref.shape[2]
    for s in range(nsub):
        rows = slice(s * sub, (s + 1) * sub)
        xs = res_ref[rows, :] + gate_ref[s] * acc[rows, :]
        x_ref[rows, :] = xs
        row_ref[j, rows, :] = xs

    @pl.when(j == nj - 1)
    def _():
        ss = jnp.sum(row_ref[0] * row_ref[0], axis=-1, keepdims=True)
        for jj in range(1, nj):
            ss = ss + jnp.sum(row_ref[jj] * row_ref[jj], axis=-1, keepdims=True)
        inv = lax.rsqrt(ss * (1.0 / (nj * tn)) + NORM_EPS)
        for jj in range(nj):
            cols = slice(jj * tn, (jj + 1) * tn)
            for s in range(nsub):
                rows = slice(s * sub, (s + 1) * sub)
                y = row_ref[jj, rows, :] * inv[rows, :] * g_ref[:, cols]
                h_ref[rows, cols] = (y * (1.0 + sc_ref[s][:, cols]) + sh_ref[s][:, cols]).astype(h_ref.dtype)


def _matmul_gated_residual(a, w, res, gate, sub, norm=None):
    m, k = a.shape
    n = w.shape[1]
    tm = _pick_tile(m, (512, 256))
    tn = _pick_tile(n, (1024, 512, 256, 128))
    nsub = tm // sub
    in_specs = [
        pl.BlockSpec((tm, k), lambda i, j: (i, 0)),
        pl.BlockSpec((k, tn), lambda i, j: (0, j)),
        pl.BlockSpec((tm, tn), lambda i, j: (i, j)),
        pl.BlockSpec((nsub, 1, tn), lambda i, j: (i, 0, j)),
    ]
    x_spec = pl.BlockSpec((tm, tn), lambda i, j: (i, j))
    x_shape = jax.ShapeDtypeStruct((m, n), F32)
    blocks = tm * k * 2 + k * tn * 2 + tm * tn * 8
    if norm is None:
        return pl.pallas_call(
            functools.partial(_mm_res_kernel, sub=sub),
            grid=(m // tm, n // tn),
            in_specs=in_specs,
            out_specs=x_spec,
            out_shape=x_shape,
            compiler_params=_params(("arbitrary", "arbitrary"), _vmem_limit(blocks, 0, tm * tn * 8)),
        )(a, w, res, gate)
    g, sc, sh = norm
    return pl.pallas_call(
        functools.partial(_mm_res_kernel, sub=sub),
        grid=(m // tm, n // tn),
        in_specs=in_specs + [
            pl.BlockSpec((1, n), lambda i, j: (0, 0)),
            pl.BlockSpec((nsub, 1, n), lambda i, j: (i, 0, 0)),
            pl.BlockSpec((nsub, 1, n), lambda i, j: (i, 0, 0)),
        ],
        out_specs=[x_spec, pl.BlockSpec((tm, n), lambda i, j: (i, 0))],
        out_shape=[x_shape, jax.ShapeDtypeStruct((m, n), BF16)],
        scratch_shapes=[pltpu.VMEM((n // tn, tm, tn), F32)],
        compiler_params=_params(("arbitrary", "arbitrary"),
                                _vmem_limit(blocks + tm * n * 2, tm * n * 4, tm * tn * 8 + tm * n * 4)),
    )(a, w, res, gate, g.reshape(1, n), sc, sh)


def _rms(x, g):
    return x * lax.rsqrt(jnp.mean(x * x, axis=-1, keepdims=True) + NORM_EPS) * g


def _rope_pair(r, cos, sin):
    half = r.shape[-1] // 2
    return r * cos + pltpu.roll(r, half, 1) * sin


def _mla_down_kernel(a_ref, w_ref, qg_ref, kvg_ref, cos_ref, sin_ref, cq_ref, ckv_ref, kr_ref, *, rq, rkv):
    acc = jnp.dot(a_ref[...], w_ref[...], preferred_element_type=F32)
    cq_ref[...] = _rms(acc[:, :rq], qg_ref[...]).astype(cq_ref.dtype)
    ckv_ref[...] = _rms(acc[:, rq:rq + rkv], kvg_ref[...]).astype(ckv_ref.dtype)
    kr_ref[...] = _rope_pair(acc[:, rq + rkv:], cos_ref[...], sin_ref[...]).astype(kr_ref.dtype)


def _mla_down(h, w_cat, q_g, kv_g, cos, sin, rq, rkv):
    m, k = h.shape
    n = w_cat.shape[1]
    nr = n - rq - rkv
    tm = _pick_tile(m, (1024, 512, 256))
    return pl.pallas_call(
        functools.partial(_mla_down_kernel, rq=rq, rkv=rkv),
        grid=(m // tm,),
        in_specs=[
            pl.BlockSpec((tm, k), lambda i: (i, 0)),
            pl.BlockSpec((k, n), lambda i: (0, 0)),
            pl.BlockSpec((1, rq), lambda i: (0, 0)),
            pl.BlockSpec((1, rkv), lambda i: (0, 0)),
            pl.BlockSpec((tm, nr), lambda i: (i, 0)),
            pl.BlockSpec((tm, nr), lambda i: (i, 0)),
        ],
        out_specs=[
            pl.BlockSpec((tm, rq), lambda i: (i, 0)),
            pl.BlockSpec((tm, rkv), lambda i: (i, 0)),
            pl.BlockSpec((tm, nr), lambda i: (i, 0)),
        ],
        out_shape=[
            jax.ShapeDtypeStruct((m, rq), BF16),
            jax.ShapeDtypeStruct((m, rkv), BF16),
            jax.ShapeDtypeStruct((m, nr), BF16),
        ],
        compiler_params=_params(("arbitrary",),
                                _vmem_limit(tm * k * 2 + k * n * 2 + tm * nr * 8 + tm * n * 2, 0, tm * n * 8)),
    )(h, w_cat, q_g.reshape(1, rq), kv_g.reshape(1, rkv), cos, sin)


def _mla_q_kernel(a_ref, w_ref, cos_ref, sin_ref, o_ref, *, hw, nope):
    acc = jnp.dot(a_ref[...], w_ref[...], preferred_element_type=F32)
    cos = cos_ref[...]
    sin = sin_ref[...]
    for hh in range(acc.shape[1] // hw):
        o_ref[:, hh * hw:hh * hw + nope] = acc[:, hh * hw:hh * hw + nope].astype(o_ref.dtype)
        o_ref[:, hh * hw + nope:(hh + 1) * hw] = _rope_pair(
            acc[:, hh * hw + nope:(hh + 1) * hw], cos, sin).astype(o_ref.dtype)


def _mla_q(cq, w_uq, cos, sin, hw, nope):
    m, k = cq.shape
    n = w_uq.shape[1]
    tm = _pick_tile(m, (1024, 512, 256))
    tn = _pick_tile(n, (1024, 512, 256))
    nr = hw - nope
    return pl.pallas_call(
        functools.partial(_mla_q_kernel, hw=hw, nope=nope),
        grid=(m // tm, n // tn),
        in_specs=[
            pl.BlockSpec((tm, k), lambda i, j: (i, 0)),
            pl.BlockSpec((k, tn), lambda i, j: (0, j)),
            pl.BlockSpec((tm, nr), lambda i, j: (i, 0)),
            pl.BlockSpec((tm, nr), lambda i, j: (i, 0)),
        ],
        out_specs=pl.BlockSpec((tm, tn), lambda i, j: (i, j)),
        out_shape=jax.ShapeDtypeStruct((m, n), BF16),
        compiler_params=_params(("arbitrary", "arbitrary"),
                                _vmem_limit(tm * k * 2 + k * tn * 2 + tm * nr * 8 + tm * tn * 2, 0, tm * tn * 8)),
    )(cq, w_uq, cos, sin)


def _scores_t(k, q):
    nq, nk = q.shape[0], k.shape[0]
    st = lax.dot_general(k, q, (((1,), (1,)), ((), ())), preferred_element_type=F32)
    rows = RED_VREGS * VREG_ELEMS // nq
    mx = jnp.max(jnp.max(st.reshape(nk // rows, rows, nq), axis=0), axis=0, keepdims=True)
    return st, mx


def _softmax_pv_t(st, mx, vt):
    nk, nq = st.shape
    rows = RED_VREGS * VREG_ELEMS // nq
    p = jnp.exp2(st.reshape(nk // rows, rows, nq) - mx)
    den = jnp.sum(jnp.sum(p, axis=0), axis=0, keepdims=True)
    ot = jnp.dot(vt, p.reshape(nk, nq).astype(BF16), preferred_element_type=F32)
    return (ot / den).T


def _attn_x_kernel(q_ref, kn_ref, kr_ref, v_ref, o_ref, kh_ref, vt_ref, s_ref, m_ref, *, n_items, tiles):
    g = pl.program_id(0)
    nope = kn_ref.shape[1]
    t_cur = lax.rem(jnp.minimum(g, n_items - 1), tiles)

    @pl.when(g == 0)
    def _():
        s_ref[1] = jnp.zeros(s_ref.shape[1:], s_ref.dtype)
        m_ref[1] = jnp.zeros(m_ref.shape[1:], m_ref.dtype)
        vt_ref[...] = jnp.zeros(vt_ref.shape, vt_ref.dtype)

    @pl.when((t_cur == 0) & (g < n_items))
    def _():
        kh_ref[:, :nope] = kn_ref[...]
        kh_ref[:, nope:] = kr_ref[...]

    @pl.when((t_cur == 1) & (g < n_items))
    def _():
        vt_ref[...] = v_ref[...].astype(F32).T.astype(BF16)

    def body(parity):
        st, mx = _scores_t(kh_ref[...], q_ref[...])
        s_ref[parity] = st
        m_ref[parity] = mx
        o_ref[...] = _softmax_pv_t(s_ref[1 - parity], m_ref[1 - parity], vt_ref[...]).astype(o_ref.dtype)

    for parity in (0, 1):
        pl.when(lax.rem(g, 2) == parity)(functools.partial(body, parity))


def _attn_ctx_kernel(q_ref, kn_ref, kr_ref, v_ref, prev_ref, o_ref):
    del prev_ref
    k = jnp.concatenate([kn_ref[...], kr_ref[...]], axis=1)
    st, mx = _scores_t(k, q_ref[...])
    vt = v_ref[...].astype(F32).T.astype(BF16)
    o_ref[...] = _softmax_pv_t(st, mx, vt).astype(o_ref.dtype)


def _attention(q, kv, kr, batch, heads, ctx, hw, nope, dv):
    m = q.shape[0]
    s_len = m // batch
    seq = s_len - ctx
    tq = ATTN_TQ
    assert seq % tq == 0 and seq % ctx == 0 and seq // tq >= 2
    tiles = seq // tq
    n_items = batch * heads * tiles
    v_col0 = heads * nope // dv
    per_batch = lambda a: a.reshape(batch, s_len, a.shape[1])
    q3, kv3, kr3 = per_batch(q), per_batch(kv), per_batch(kr)

    def item(g):
        t = lax.rem(g, tiles)
        bh = g // tiles
        return bh // heads, lax.rem(bh, heads), t

    def cur(g):
        return item(jnp.minimum(g, n_items - 1))

    def prev(g):
        return item(jnp.maximum(g - 1, 0))

    out = pl.pallas_call(
        functools.partial(_attn_x_kernel, n_items=n_items, tiles=tiles),
        grid=(n_items + 1,),
        in_specs=[
            pl.BlockSpec((None, tq, hw), lambda g: (cur(g)[0], cur(g)[2], cur(g)[1])),
            pl.BlockSpec((None, s_len, nope), lambda g: (cur(g)[0], 0, cur(g)[1])),
            pl.BlockSpec((None, s_len, hw - nope), lambda g: (cur(g)[0], 0, 0)),
            pl.BlockSpec((None, s_len, dv), lambda g: (cur(g)[0], 0, v_col0 + cur(g)[1])),
        ],
        out_specs=pl.BlockSpec((None, tq, dv), lambda g: (prev(g)[0], prev(g)[2], prev(g)[1])),
        out_shape=jax.ShapeDtypeStruct((batch, s_len, heads * dv), BF16),
        scratch_shapes=[pltpu.VMEM((s_len, hw), BF16), pltpu.VMEM((dv, s_len), BF16),
                        pltpu.VMEM((2, s_len, tq), F32), pltpu.VMEM((2, 1, tq), F32)],
        compiler_params=_params(("arbitrary",),
                                _vmem_limit(tq * hw * 2 + s_len * (hw + dv) * 2 + tq * dv * 2,
                                            s_len * (hw + dv) * 2 + 2 * s_len * tq * 4, tq * s_len * 8)),
    )(q3, kv3, kr3, kv3)

    cb = seq // ctx
    out = pl.pallas_call(
        _attn_ctx_kernel,
        grid=(batch, heads),
        in_specs=[
            pl.BlockSpec((None, ctx, hw), lambda b, h: (b, cb, h)),
            pl.BlockSpec((None, ctx, nope), lambda b, h: (b, cb, h)),
            pl.BlockSpec((None, ctx, hw - nope), lambda b, h: (b, cb, 0)),
            pl.BlockSpec((None, ctx, dv), lambda b, h: (b, cb, v_col0 + h)),
            pl.BlockSpec(memory_space=pl.ANY),
        ],
        out_specs=pl.BlockSpec((None, ctx, dv), lambda b, h: (b, cb, h)),
        out_shape=jax.ShapeDtypeStruct((batch, s_len, heads * dv), BF16),
        input_output_aliases={4: 0},
        compiler_params=_params(("arbitrary", "arbitrary"), _vmem_limit(ctx * (2 * hw + 2 * dv) * 2, 0, 4 << 20)),
    )(q3, kv3, kr3, kv3, out)
    return out.reshape(m, heads * dv)


def _ret_qk_kernel(a_ref, w_ref, cos_ref, sin_ref, o_ref, *, dk, scale):
    acc = jnp.dot(a_ref[...], w_ref[...], preferred_element_type=F32)
    cos = cos_ref[...]
    sin = sin_ref[...]
    half = dk // 2
    for hh in range(acc.shape[1] // dk):
        x1 = acc[:, hh * dk:hh * dk + half]
        x2 = acc[:, hh * dk + half:(hh + 1) * dk]
        o_ref[:, hh * dk:hh * dk + half] = ((x1 * cos - x2 * sin) * scale).astype(o_ref.dtype)
        o_ref[:, hh * dk + half:(hh + 1) * dk] = ((x2 * cos + x1 * sin) * scale).astype(o_ref.dtype)


def _ret_qk(h, w, cos, sin, dk, scale, out_dtype):
    m, k = h.shape
    n = w.shape[1]
    tm = _pick_tile(m, (1024, 512, 256))
    tn = _pick_tile(n, (1024, 512, 256))
    half = dk // 2
    osz = jnp.dtype(out_dtype).itemsize
    return pl.pallas_call(
        functools.partial(_ret_qk_kernel, dk=dk, scale=scale),
        grid=(m // tm, n // tn),
        in_specs=[
            pl.BlockSpec((tm, k), lambda i, j: (i, 0)),
            pl.BlockSpec((k, tn), lambda i, j: (0, j)),
            pl.BlockSpec((tm, half), lambda i, j: (i, 0)),
            pl.BlockSpec((tm, half), lambda i, j: (i, 0)),
        ],
        out_specs=pl.BlockSpec((tm, tn), lambda i, j: (i, j)),
        out_shape=jax.ShapeDtypeStruct((m, n), out_dtype),
        compiler_params=_params(("arbitrary", "arbitrary"),
                                _vmem_limit(tm * k * 2 + k * tn * 2 + tm * half * 8 + tm * tn * osz, 0, tm * tn * 8)),
    )(h, w, cos, sin)


def _head_norm(y):
    d = y - jnp.mean(y, axis=-1, keepdims=True)
    return d * lax.rsqrt(jnp.mean(d * d, axis=-1, keepdims=True) + GN_EPS)


def _ret_chunk_kernel(cd_ref, q_ref, k_ref, v_ref, intra_ref, xi_ref, zeta_ref, g_ref, *rest, heads):
    if len(rest) == 3:
        prev_ref, y_ref, state_ref = rest
    else:
        prev_ref = None
        y_ref, state_ref = rest
    t = pl.program_id(1)

    @pl.when(t == 0)
    def _():
        state_ref[...] = jnp.zeros_like(state_ref)

    dk = q_ref.shape[1] // heads
    dv = v_ref.shape[1] // heads
    for h in range(heads):
        q = q_ref[:, h * dk:(h + 1) * dk]
        kf = k_ref[:, h * dk:(h + 1) * dk]
        v = v_ref[:, h * dv:(h + 1) * dv]
        st = state_ref[h]
        s = lax.dot_general(q, kf.astype(BF16), (((1,), (1,)), ((), ())),
                            preferred_element_type=F32) * intra_ref[h]
        y = jnp.dot(s.astype(BF16), v, preferred_element_type=F32)
        y = y + jnp.dot(q, st.astype(BF16), preferred_element_type=F32) * jnp.tile(xi_ref[h], (1, dv // LANES))
        cols = slice(h * dv, (h + 1) * dv)
        out = g_ref[:, cols].astype(F32) * _head_norm(y)
        if prev_ref is not None:
            out = prev_ref[:, cols] + out
        y_ref[:, cols] = out.astype(y_ref.dtype)
        kz = (kf * jnp.tile(zeta_ref[h], (1, dk // LANES))).astype(BF16)
        upd = lax.dot_general(kz, v, (((0,), (0,)), ((), ())), preferred_element_type=F32)
        state_ref[h] = st * cd_ref[h] + upd


def _ret_chunks(q, k, v, gate, prev, tables, batch, heads, ctx, backward):
    m = q.shape[0]
    c = SCAN_CHUNK
    assert ctx % c == 0 and (m // batch) % c == 0
    nch = m // batch // c
    ncc = ctx // c
    cd, intra, xi, zeta = tables
    dkh = q.shape[1]
    dvh = v.shape[1]

    nxc = nch - ncc
    if backward:
        def chunk(t):
            return nch - 1 - t
    else:
        def chunk(t):
            return jnp.where(t < ncc, nxc + t, t - ncc)

    row = lambda b, t: (b * nch + chunk(t), 0)
    const3 = lambda b, t: (0, 0, 0)
    extra = [gate] if prev is None else [gate, prev]
    return pl.pallas_call(
        functools.partial(_ret_chunk_kernel, heads=heads),
        grid=(batch, nch),
        in_specs=[
            pl.BlockSpec(memory_space=pltpu.SMEM),
            pl.BlockSpec((c, dkh), row),
            pl.BlockSpec((c, dkh), row),
            pl.BlockSpec((c, dvh), row),
            pl.BlockSpec(intra.shape, const3),
            pl.BlockSpec(xi.shape, const3),
            pl.BlockSpec(zeta.shape, const3),
        ] + [pl.BlockSpec((c, dvh), row)] * len(extra),
        out_specs=pl.BlockSpec((c, dvh), row),
        out_shape=jax.ShapeDtypeStruct((m, dvh), F32 if prev is None else BF16),
        scratch_shapes=[pltpu.VMEM((heads, dkh // heads, dvh // heads), F32)],
        compiler_params=_params(("arbitrary", "arbitrary"),
                                _vmem_limit(c * dkh * 6 + c * dvh * 12 + (intra.size + xi.size + zeta.size) * 4,
                                            dkh * dvh // heads * 4, 8 << 20)),
    )(cd, q, k, v, intra, xi, zeta, *extra)


def _ret_tables(decay, backward):
    c = SCAN_CHUNK
    lg = -jnp.exp(decay.astype(F32))
    pos = jnp.arange(c, dtype=F32)
    diff = pos[:, None] - pos[None, :]
    lower = diff >= 0
    intra = jnp.where(lower, jnp.exp(lg[:, None, None] * jnp.where(lower, diff, 0.0)), 0.0)
    xi = jnp.exp(lg[:, None] * (pos + 1.0))
    zeta = jnp.exp(lg[:, None] * (c - 1.0 - pos))
    cd = jnp.exp(lg * c)
    if backward:
        intra = jnp.flip(intra, axis=(1, 2))
        xi = jnp.flip(xi, axis=1)
        zeta = jnp.flip(zeta, axis=1)
    heads = decay.shape[0]
    xi = jnp.broadcast_to(xi[:, :, None], (heads, c, LANES))
    zeta = jnp.broadcast_to(zeta[:, :, None], (heads, c, LANES))
    return cd, intra, xi, zeta


def _ffn_gu_kernel(a_ref, ap_ref, an_ref, wg_ref, wu_ref, cw_ref, cb_ref, o_ref, aext_ref, *, s_len, sub):
    i = pl.program_id(0)
    tm = a_ref.shape[0]
    halo = ap_ref.shape[0]

    @pl.when(pl.program_id(1) == 0)
    def _():
        aext_ref[:halo, :] = ap_ref[...]
        aext_ref[halo:halo + tm, :] = a_ref[...]
        aext_ref[halo + tm:, :] = an_ref[...]

    g_ext = jnp.dot(aext_ref[...], wg_ref[...], preferred_element_type=F32)
    u = jnp.dot(a_ref[...], wu_ref[...], preferred_element_type=F32)
    rows = g_ext.shape[0]
    g_prev = pltpu.roll(g_ext, 1, 0)[halo:halo + tm]
    g_next = pltpu.roll(g_ext, rows - 1, 0)[halo:halo + tm]
    g_cur = g_ext[halo:halo + tm]

    r = (i * tm + lax.broadcasted_iota(jnp.int32, (tm, 1), 0)).astype(F32)
    rb = r - jnp.floor((r + 0.5) * (1.0 / s_len)) * s_len
    seq = s_len - sub
    g_prev = jnp.where((rb == 0.0) | (rb == float(seq)), 0.0, g_prev)
    g_next = jnp.where((rb == float(seq - 1)) | (rb == float(s_len - 1)), 0.0, g_next)
    g = g_prev * cw_ref[0:1, :] + g_cur * cw_ref[1:2, :] + g_next * cw_ref[2:3, :] + cb_ref[...]
    o_ref[...] = (_silu(g) * u).astype(o_ref.dtype)


def _ffn_gate_up(h, w_gate, w_up, conv_w, conv_b, s_len, sub):
    m, k = h.shape
    n = w_gate.shape[1]
    tm = _pick_tile(m, (1024, 512, 256))
    tn = _pick_tile(n, (512, 256, 128))
    halo = BF16_SUBLANES
    nhb = m // halo
    return pl.pallas_call(
        functools.partial(_ffn_gu_kernel, s_len=s_len, sub=sub),
        grid=(m // tm, n // tn),
        in_specs=[
            pl.BlockSpec((tm, k), lambda i, j: (i, 0)),
            pl.BlockSpec((halo, k), lambda i, j: (jnp.maximum(i * (tm // halo) - 1, 0), 0)),
            pl.BlockSpec((halo, k), lambda i, j: (jnp.minimum((i + 1) * (tm // halo), nhb - 1), 0)),
            pl.BlockSpec((k, tn), lambda i, j: (0, j)),
            pl.BlockSpec((k, tn), lambda i, j: (0, j)),
            pl.BlockSpec((CONV_W, tn), lambda i, j: (0, j)),
            pl.BlockSpec((1, tn), lambda i, j: (0, j)),
        ],
        out_specs=pl.BlockSpec((tm, tn), lambda i, j: (i, j)),
        out_shape=jax.ShapeDtypeStruct((m, n), BF16),
        scratch_shapes=[pltpu.VMEM((tm + 2 * halo, k), BF16)],
        compiler_params=_params(("arbitrary", "arbitrary"),
                                _vmem_limit((tm + 2 * halo) * k * 2 + 2 * k * tn * 2 + tm * tn * 2,
                                            (tm + 2 * halo) * k * 2, tm * tn * 32)),
    )(h, h, h, w_gate, w_up, conv_w, conv_b.reshape(1, n))


def _expand_rows(tab_x, ctx_fill, batch, ctx):
    n = tab_x.shape[1]
    full = jnp.concatenate([tab_x, jnp.broadcast_to(ctx_fill, (ctx, n)).astype(F32)], axis=0)
    return jnp.tile(full, (batch, 1))


def _mla_rope_tables(seq, batch, ctx):
    rows = seq // GRID_W
    row = jnp.repeat(jnp.arange(rows), GRID_W).astype(F32)
    col = jnp.tile(jnp.arange(GRID_W), rows).astype(F32)
    axis_dim = MLA_ROPE // 2
    m = axis_dim // 2
    inv_axis = ROPE_BASE ** (-jnp.arange(0, axis_dim, 2, dtype=F32) / axis_dim)
    cos_parts, sin_parts = [], []
    for ang in (row[:, None] * inv_axis[None], col[:, None] * inv_axis[None]):
        c, s = jnp.cos(ang), jnp.sin(ang)
        cos_parts += [c, c]
        sin_parts += [-s, s]
    zeros = jnp.zeros((seq, MLA_ROPE), F32)
    cos = jnp.concatenate(cos_parts + [zeros], axis=1)
    sin = jnp.concatenate(sin_parts + [zeros], axis=1)
    fill_cos = jnp.concatenate([jnp.ones((MLA_ROPE,), F32), jnp.zeros((MLA_ROPE,), F32)])
    cos = _expand_rows(cos, fill_cos, batch, ctx)
    sin = _expand_rows(sin, jnp.zeros((2 * MLA_ROPE,), F32), batch, ctx)
    partner = []
    for base in (0, axis_dim):
        partner += [base + m + d for d in range(m)] + [base + d for d in range(m)]
    return cos, sin, jnp.array(partner, jnp.int32)


def _ret_rope_tables(seq, dk, batch, ctx):
    inv_ret = ROPE_BASE ** (-jnp.arange(0, dk, 2, dtype=F32) / dk)
    ang = jnp.arange(seq, dtype=F32)[:, None] * inv_ret[None]
    cos = _expand_rows(jnp.cos(ang), jnp.ones((dk // 2,), F32), batch, ctx)
    sin = _expand_rows(jnp.sin(ang), jnp.zeros((dk // 2,), F32), batch, ctx)
    return cos, sin


def _sub_table(vec_x, vec_c, nsb):
    b, d = vec_x.shape
    full = jnp.concatenate([jnp.broadcast_to(vec_x[:, None, :], (b, nsb - 1, d)),
                            jnp.broadcast_to(vec_c[None, None, :], (b, 1, d))], axis=1)
    return full.reshape(b * nsb, 1, d)


def _mla_layer(h, xc, gate, norm, tabs, batch, ctx, w_dq, q_g, w_uq, w_dkv, kv_g, w_ukv, w_o):
    cos, sin, partner = tabs
    rq = w_dq.shape[1]
    rkv = kv_g.shape[0]
    heads, nope, rope, dv = MLA_HEADS, MLA_NOPE, MLA_ROPE, MLA_V
    hw = nope + 2 * rope
    w_kr = w_dkv[:, rkv:]
    w_cat = jnp.concatenate([w_dq, w_dkv[:, :rkv], w_kr, w_kr[:, partner]], axis=1).astype(BF16)
    cq, ckv, kr = _mla_down(h, w_cat, q_g, kv_g, cos, sin, rq, rkv)

    scale = float(nope + rope) ** -0.5 * 1.4426950408889634
    wq = w_uq.reshape(rq, heads, nope + rope) * scale
    wq = jnp.concatenate([wq, wq[:, :, nope:][:, :, partner]], axis=2).reshape(rq, heads * hw).astype(BF16)
    q = _mla_q(cq, wq, cos, sin, hw, nope)

    wkv = w_ukv.reshape(rkv, heads, nope + dv)
    wkv = jnp.concatenate([wkv[:, :, :nope].reshape(rkv, heads * nope),
                           wkv[:, :, nope:].reshape(rkv, heads * dv)], axis=1).astype(BF16)
    kv = _matmul(ckv, wkv, BF16)

    o = _attention(q, kv, kr, batch, heads, ctx, hw, nope, dv)
    return _matmul_gated_residual(o, w_o.astype(BF16), xc, gate, ctx, norm)


def _ret_layer(h, xc, gate, norm, tabs, batch, ctx, w_q, w_k, w_v, w_gf, w_gb, w_o, decay_f, decay_b):
    cos, sin = tabs
    heads = RET_HEADS
    dk = w_q.shape[1] // heads
    q = _ret_qk(h, w_q.astype(BF16), cos, sin, dk, 1.0, BF16)
    k = _ret_qk(h, w_k.astype(BF16), cos, sin, dk, float(dk) ** -0.5, F32)
    v = _matmul(h, w_v.astype(BF16), BF16)
    gf = _matmul(h, w_gf.astype(BF16), BF16, act="silu")
    gb = _matmul(h, w_gb.astype(BF16), BF16, act="silu")
    fwd = _ret_chunks(q, k, v, gf, None, _ret_tables(decay_f, False), batch, heads, ctx, False)
    mix = _ret_chunks(q, k, v, gb, fwd, _ret_tables(decay_b, True), batch, heads, ctx, True)
    return _matmul_gated_residual(mix, w_o.astype(BF16), xc, gate, ctx, norm)


def kernel(x, c, ctx, c_ctx, mod_w, mod_b, norm_mix_g, norm_ffn_g, mla_w_dq, mla_q_norm_g, mla_w_uq, mla_w_dkv, mla_kv_norm_g, mla_w_ukv, mla_w_o, ret_w_q, ret_w_k, ret_w_v, ret_w_gf, ret_w_gb, ret_w_o, ret_decay_f, ret_decay_b, ffn_w_gate, ffn_w_up, ffn_conv_w, ffn_conv_b, ffn_w_down, final_norm_g):
    batch, seq, d = x.shape
    ctx_len = ctx.shape[1]
    depth = mod_w.shape[0]
    s_len = ctx_len + seq
    nsb = s_len // ctx_len
    assert seq % ctx_len == 0 and ctx_len % RET_CHUNK == 0 and seq % GRID_W == 0
    assert mod_w.shape[2] == N_MOD * d

    xc = jnp.concatenate([x, ctx], axis=1).reshape(batch * s_len, d)

    cond_rows = 2 * 8
    cond = jnp.zeros((cond_rows, d), F32).at[:batch].set(c).at[batch].set(c_ctx)
    assert batch + 1 <= cond_rows
    mod = _modulation(cond, mod_w, mod_b)

    mla_tabs = _mla_rope_tables(seq, batch, ctx_len)
    ret_tabs = _ret_rope_tables(seq, ret_w_q.shape[2] // RET_HEADS, batch, ctx_len)

    parts = [[_sub_table(mod[l, :batch, k * d:(k + 1) * d], mod[l, batch, k * d:(k + 1) * d], nsb)
              for k in range(N_MOD)] for l in range(depth)]

    h = _norm_mod(xc, norm_mix_g[0], parts[0][1], parts[0][0], ctx_len)
    for l in range(depth):
        i = l // N_MIXERS
        _, _, g1, sh2, sc2, g2 = parts[l]
        ffn_norm = (norm_ffn_g[l], sc2, sh2)
        if l % N_MIXERS == 0:
            xc, h2 = _mla_layer(h, xc, g1, ffn_norm, mla_tabs, batch, ctx_len, mla_w_dq[i], mla_q_norm_g[i],
                                mla_w_uq[i], mla_w_dkv[i], mla_kv_norm_g[i], mla_w_ukv[i], mla_w_o[i])
        else:
            xc, h2 = _ret_layer(h, xc, g1, ffn_norm, ret_tabs, batch, ctx_len, ret_w_q[i], ret_w_k[i], ret_w_v[i],
                                ret_w_gf[i], ret_w_gb[i], ret_w_o[i], ret_decay_f[i], ret_decay_b[i])
        act = _ffn_gate_up(h2, ffn_w_gate[l].astype(BF16), ffn_w_up[l].astype(BF16),
                           ffn_conv_w[l], ffn_conv_b[l], s_len, ctx_len)
        w_down = ffn_w_down[l].astype(BF16)
        if l + 1 < depth:
            xc, h = _matmul_gated_residual(act, w_down, xc, g2, ctx_len,
                                           (norm_mix_g[l + 1], parts[l + 1][1], parts[l + 1][0]))
        else:
            xc = _matmul_gated_residual(act, w_down, xc, g2, ctx_len)

    return _final_norm(xc, final_norm_g, batch, seq, ctx_len).reshape(batch, seq, d)
```

```python
import functools

import jax
import jax.numpy as jnp
from jax import lax
from jax.experimental import pallas as pl
from jax.experimental.pallas import tpu as pltpu

GRID_W = 64
N_MIXERS = 2
MLA_HEADS = 16
MLA_NOPE = 128
MLA_ROPE = 64
MLA_V = 128
RET_HEADS = 8
RET_CHUNK = 128
CONV_W = 3
N_MOD = 6
ROPE_BASE = 10000.0
NORM_EPS = 1e-6
GN_EPS = 1e-5

V7X_VMEM_BYTES = 64 * 1024 * 1024
LANES = 128
BF16_SUBLANES = 16
MXU_COL = 256
VREG_ELEMS = 8 * LANES
RED_VREGS = 16
ATTN_TQ = 512
SCAN_CHUNK = MXU_COL

F32 = jnp.float32
BF16 = jnp.bfloat16


VMEM_CAP = V7X_VMEM_BYTES - (6 << 20)
ROW_TILES = (2048, 1024, 512, 256)
COL_TILES = (1024, 512, 256, 128)


def _vmem_need(block_bytes, scratch_bytes=0, temp_bytes=0):
    return 2 * block_bytes + scratch_bytes + temp_bytes + (4 << 20)


def _vmem_limit(block_bytes, scratch_bytes=0, temp_bytes=0):
    return int(min(_vmem_need(block_bytes, scratch_bytes, temp_bytes), VMEM_CAP))


def _fit_tiles(m, n, tms, tns, need):
    for tm in tms:
        for tn in tns:
            if m % tm == 0 and n % tn == 0 and _vmem_need(*need(tm, tn)) <= VMEM_CAP:
                return tm, tn
    raise ValueError(f"no tile of {tms} x {tns} fits ({m}, {n})")


def _params(sem, limit, flags=None):
    return pltpu.CompilerParams(dimension_semantics=sem, vmem_limit_bytes=limit, flags=flags)


def _pick_tile(n, prefs):
    for t in prefs:
        if n % t == 0:
            return t
    raise ValueError(f"no tile in {prefs} divides {n}")


def _silu(x):
    return x * (1.0 / (1.0 + jnp.exp(-x)))


def _mod_kernel(a_ref, w_ref, b_ref, o_ref):
    a = _silu(a_ref[...]).astype(BF16)
    acc = jnp.dot(a, w_ref[0].astype(BF16), preferred_element_type=F32)
    o_ref[0] = acc + b_ref[0]


def _modulation(cond, mod_w, mod_b):
    depth, d, n = mod_w.shape
    rows = cond.shape[0]
    tn = _pick_tile(n, (1024, 512, 256, 128))
    return pl.pallas_call(
        _mod_kernel,
        grid=(depth, n // tn),
        in_specs=[
            pl.BlockSpec((rows, d), lambda l, j: (0, 0)),
            pl.BlockSpec((1, d, tn), lambda l, j: (l, 0, j)),
            pl.BlockSpec((1, 1, tn), lambda l, j: (l, 0, j)),
        ],
        out_specs=pl.BlockSpec((1, rows, tn), lambda l, j: (l, 0, j)),
        out_shape=jax.ShapeDtypeStruct((depth, rows, n), F32),
        compiler_params=_params(("arbitrary", "arbitrary"),
                                _vmem_limit(d * tn * 4 + rows * (d + tn) * 4, 0, d * tn * 2)),
    )(cond, mod_w, mod_b.reshape(depth, 1, n))


def _embed_norm_kernel(x_ref, c_ref, g_ref, sc_ref, sh_ref, xc_ref, h_ref, *, nx):
    x = jnp.where(pl.program_id(1) < nx, x_ref[...], c_ref[...])
    xc_ref[...] = x
    y = x * lax.rsqrt(jnp.mean(x * x, axis=-1, keepdims=True) + NORM_EPS) * g_ref[...]
    h_ref[...] = (y * (1.0 + sc_ref[0]) + sh_ref[0]).astype(h_ref.dtype)


def _embed_norm(x, ctx, g, sc, sh):
    batch, seq, d = x.shape
    sub = ctx.shape[1]
    nx = seq // sub
    nsb = nx + 1
    m = batch * nsb * sub
    row = lambda b, j: (b * nsb + j, 0)
    tab = lambda b, j: (b * nsb + j, 0, 0)
    return pl.pallas_call(
        functools.partial(_embed_norm_kernel, nx=nx),
        grid=(batch, nsb),
        in_specs=[
            pl.BlockSpec((None, sub, d), lambda b, j: (b, jnp.minimum(j, nx - 1), 0)),
            pl.BlockSpec((None, sub, d), lambda b, j: (b, 0, 0)),
            pl.BlockSpec((1, d), lambda b, j: (0, 0)),
            pl.BlockSpec((1, 1, d), tab),
            pl.BlockSpec((1, 1, d), tab),
        ],
        out_specs=[pl.BlockSpec((sub, d), row), pl.BlockSpec((sub, d), row)],
        out_shape=[jax.ShapeDtypeStruct((m, d), F32), jax.ShapeDtypeStruct((m, d), BF16)],
        compiler_params=_params(("arbitrary", "arbitrary"), _vmem_limit(sub * d * 14, 0, sub * d * 8)),
    )(x, ctx, g.reshape(1, d), sc, sh)


def _final_norm_kernel(x_ref, g_ref, o_ref):
    x = x_ref[...]
    o_ref[...] = x * lax.rsqrt(jnp.mean(x * x, axis=-1, keepdims=True) + NORM_EPS) * g_ref[...]


def _final_norm(xc, g, batch, seq, sub):
    m, d = xc.shape
    nsb = m // batch // sub
    nx = seq // sub
    return pl.pallas_call(
        _final_norm_kernel,
        grid=(batch, nx),
        in_specs=[
            pl.BlockSpec((sub, d), lambda b, j: (b * nsb + j, 0)),
            pl.BlockSpec((1, d), lambda b, j: (0, 0)),
        ],
        out_specs=pl.BlockSpec((sub, d), lambda b, j: (b * nx + j, 0)),
        out_shape=jax.ShapeDtypeStruct((batch * seq, d), F32),
        compiler_params=_params(("arbitrary", "arbitrary"), _vmem_limit(sub * d * 8, 0, sub * d * 8)),
    )(xc, g.reshape(1, d))


def _mm_kernel(a_ref, w_ref, o_ref, *, act):
    acc = jnp.dot(a_ref[...], w_ref[...], preferred_element_type=F32)
    if act == "silu":
        acc = _silu(acc)
    o_ref[...] = acc.astype(o_ref.dtype)


def _weight(w):
    arr = w[0] if isinstance(w, tuple) else w
    return arr, arr.shape[-2], arr.shape[-1]


def _weight_spec(w, tn):
    if isinstance(w, tuple):
        arr, layer = w
        return pl.BlockSpec((None, arr.shape[1], tn), lambda i, j: (layer, 0, j))
    return pl.BlockSpec((w.shape[0], tn), lambda i, j: (0, j))


def _matmul(a, w, out_dtype, act=None):
    m, k = a.shape
    w_arr, _, n = _weight(w)
    osz = jnp.dtype(out_dtype).itemsize
    need = lambda tm, tn: (tm * k * 2 + k * tn * 2 + tm * tn * osz, 0, tm * tn * 8)
    tm, tn = _fit_tiles(m, n, ROW_TILES, COL_TILES, need)
    return pl.pallas_call(
        functools.partial(_mm_kernel, act=act),
        grid=(m // tm, n // tn),
        in_specs=[
            pl.BlockSpec((tm, k), lambda i, j: (i, 0)),
            _weight_spec(w, tn),
        ],
        out_specs=pl.BlockSpec((tm, tn), lambda i, j: (i, j)),
        out_shape=jax.ShapeDtypeStruct((m, n), out_dtype),
        compiler_params=_params(("arbitrary", "arbitrary"), _vmem_limit(*need(tm, tn))),
    )(a, w_arr)


def _mm_res_kernel(a_ref, w_ref, res_ref, gate_ref, *rest, sub):
    acc = jnp.dot(a_ref[...], w_ref[...], preferred_element_type=F32)
    nsub = acc.shape[0] // sub
    if len(rest) == 1:
        (x_ref,) = rest
        for s in range(nsub):
            rows = slice(s * sub, (s + 1) * sub)
            x_ref[rows, :] = res_ref[rows, :] + gate_ref[s] * acc[rows, :]
        return

    g_ref, sc_ref, sh_ref, x_ref, h_ref, row_ref = rest
    j = pl.program_id(1)
    nj = row_ref.shape[0]
    tn = row_ref.shape[2]
    for s in range(nsub):
        rows = slice(s * sub, (s + 1) * sub)
        xs = res_ref[rows, :] + gate_ref[s] * acc[rows, :]
        x_ref[rows, :] = xs
        row_ref[j, rows, :] = xs

    @pl.when(j == nj - 1)
    def _():
        ss = jnp.sum(row_ref[0] * row_ref[0], axis=-1, keepdims=True)
        for jj in range(1, nj):
            ss = ss + jnp.sum(row_ref[jj] * row_ref[jj], axis=-1, keepdims=True)
        inv = lax.rsqrt(ss * (1.0 / (nj * tn)) + NORM_EPS)
        for jj in range(nj):
            cols = slice(jj * tn, (jj + 1) * tn)
            for s in range(nsub):
                rows = slice(s * sub, (s + 1) * sub)
                y = row_ref[jj, rows, :] * inv[rows, :] * g_ref[:, cols]
                h_ref[rows, cols] = (y * (1.0 + sc_ref[s][:, cols]) + sh_ref[s][:, cols]).astype(h_ref.dtype)


def _matmul_gated_residual(a, w, res, gate, sub, norm=None):
    m, k = a.shape
    w_arr, _, n = _weight(w)
    fused = norm is not None

    def need(tm, tn):
        blocks = tm * k * 2 + k * tn * 2 + tm * tn * 8
        return (blocks + tm * n * 2, tm * n * 4, tm * tn * 4) if fused else (blocks, 0, tm * tn * 4)

    tm, tn = _fit_tiles(m, n, tuple(t for t in ROW_TILES if t <= 512), (1024, 2 * MXU_COL), need)
    nsub = tm // sub
    in_specs = [
        pl.BlockSpec((tm, k), lambda i, j: (i, 0)),
        _weight_spec(w, tn),
        pl.BlockSpec((tm, tn), lambda i, j: (i, j)),
        pl.BlockSpec((nsub, 1, tn), lambda i, j: (i, 0, j)),
    ]
    x_spec = pl.BlockSpec((tm, tn), lambda i, j: (i, j))
    x_shape = jax.ShapeDtypeStruct((m, n), F32)
    params = _params(("arbitrary", "arbitrary"), _vmem_limit(*need(tm, tn)))
    if not fused:
        return pl.pallas_call(
            functools.partial(_mm_res_kernel, sub=sub),
            grid=(m // tm, n // tn),
            in_specs=in_specs,
            out_specs=x_spec,
            out_shape=x_shape,
            compiler_params=params,
        )(a, w_arr, res, gate)
    g, sc, sh = norm
    return pl.pallas_call(
        functools.partial(_mm_res_kernel, sub=sub),
        grid=(m // tm, n // tn),
        in_specs=in_specs + [
            pl.BlockSpec((1, n), lambda i, j: (0, 0)),
            pl.BlockSpec((nsub, 1, n), lambda i, j: (i, 0, 0)),
            pl.BlockSpec((nsub, 1, n), lambda i, j: (i, 0, 0)),
        ],
        out_specs=[x_spec, pl.BlockSpec((tm, n), lambda i, j: (i, 0))],
        out_shape=[x_shape, jax.ShapeDtypeStruct((m, n), BF16)],
        scratch_shapes=[pltpu.VMEM((n // tn, tm, tn), F32)],
        compiler_params=params,
    )(a, w_arr, res, gate, g.reshape(1, n), sc, sh)


def _rms(x, g):
    return x * lax.rsqrt(jnp.mean(x * x, axis=-1, keepdims=True) + NORM_EPS) * g


def _rope_pair(r, cos, sin):
    half = r.shape[-1] // 2
    return r * cos + pltpu.roll(r, half, 1) * sin


def _mla_down_kernel(a_ref, w_ref, qg_ref, kvg_ref, cos_ref, sin_ref, cq_ref, ckv_ref, kr_ref, *, rq, rkv):
    acc = jnp.dot(a_ref[...], w_ref[...], preferred_element_type=F32)
    cq_ref[...] = _rms(acc[:, :rq], qg_ref[...]).astype(cq_ref.dtype)
    ckv_ref[...] = _rms(acc[:, rq:rq + rkv], kvg_ref[...]).astype(ckv_ref.dtype)
    kr_ref[...] = _rope_pair(acc[:, rq + rkv:], cos_ref[...], sin_ref[...]).astype(kr_ref.dtype)


def _mla_down(h, w_cat, q_g, kv_g, cos, sin, rq, rkv):
    m, k = h.shape
    n = w_cat.shape[1]
    nr = n - rq - rkv
    tm = _pick_tile(m, (1024, 512, 256))
    return pl.pallas_call(
        functools.partial(_mla_down_kernel, rq=rq, rkv=rkv),
        grid=(m // tm,),
        in_specs=[
            pl.BlockSpec((tm, k), lambda i: (i, 0)),
            pl.BlockSpec((k, n), lambda i: (0, 0)),
            pl.BlockSpec((1, rq), lambda i: (0, 0)),
            pl.BlockSpec((1, rkv), lambda i: (0, 0)),
            pl.BlockSpec((tm, nr), lambda i: (i, 0)),
            pl.BlockSpec((tm, nr), lambda i: (i, 0)),
        ],
        out_specs=[
            pl.BlockSpec((tm, rq), lambda i: (i, 0)),
            pl.BlockSpec((tm, rkv), lambda i: (i, 0)),
            pl.BlockSpec((tm, nr), lambda i: (i, 0)),
        ],
        out_shape=[
            jax.ShapeDtypeStruct((m, rq), BF16),
            jax.ShapeDtypeStruct((m, rkv), BF16),
            jax.ShapeDtypeStruct((m, nr), BF16),
        ],
        compiler_params=_params(("arbitrary",),
                                _vmem_limit(tm * k * 2 + k * n * 2 + tm * nr * 8 + tm * n * 2, 0, tm * n * 8)),
    )(h, w_cat, q_g.reshape(1, rq), kv_g.reshape(1, rkv), cos, sin)


def _mla_q_kernel(a_ref, w_ref, cos_ref, sin_ref, o_ref, *, hw, nope):
    acc = jnp.dot(a_ref[...], w_ref[...], preferred_element_type=F32)
    cos = cos_ref[...]
    sin = sin_ref[...]
    for hh in range(acc.shape[1] // hw):
        o_ref[:, hh * hw:hh * hw + nope] = acc[:, hh * hw:hh * hw + nope].astype(o_ref.dtype)
        o_ref[:, hh * hw + nope:(hh + 1) * hw] = _rope_pair(
            acc[:, hh * hw + nope:(hh + 1) * hw], cos, sin).astype(o_ref.dtype)


def _mla_q(cq, w_uq, cos, sin, hw, nope):
    m, k = cq.shape
    n = w_uq.shape[1]
    tm = _pick_tile(m, (1024, 512, 256))
    tn = _pick_tile(n, (1024, 512, 256))
    nr = hw - nope
    return pl.pallas_call(
        functools.partial(_mla_q_kernel, hw=hw, nope=nope),
        grid=(m // tm, n // tn),
        in_specs=[
            pl.BlockSpec((tm, k), lambda i, j: (i, 0)),
            pl.BlockSpec((k, tn), lambda i, j: (0, j)),
            pl.BlockSpec((tm, nr), lambda i, j: (i, 0)),
            pl.BlockSpec((tm, nr), lambda i, j: (i, 0)),
        ],
        out_specs=pl.BlockSpec((tm, tn), lambda i, j: (i, j)),
        out_shape=jax.ShapeDtypeStruct((m, n), BF16),
        compiler_params=_params(("arbitrary", "arbitrary"),
                                _vmem_limit(tm * k * 2 + k * tn * 2 + tm * nr * 8 + tm * tn * 2, 0, tm * tn * 8)),
    )(cq, w_uq, cos, sin)


def _scores_t(k, q):
    nq, nk = q.shape[0], k.shape[0]
    st = lax.dot_general(k, q, (((1,), (1,)), ((), ())), preferred_element_type=F32)
    rows = RED_VREGS * VREG_ELEMS // nq
    mx = jnp.max(jnp.max(st.reshape(nk // rows, rows, nq), axis=0), axis=0, keepdims=True)
    return st, mx


def _softmax_pv_t(st, mx, vt):
    nk, nq = st.shape
    rows = RED_VREGS * VREG_ELEMS // nq
    p = jnp.exp2(st.reshape(nk // rows, rows, nq) - mx)
    den = jnp.sum(jnp.sum(p, axis=0), axis=0, keepdims=True)
    ot = jnp.dot(vt, p.reshape(nk, nq).astype(BF16), preferred_element_type=F32)
    return (ot / den).T


def _attn_x_kernel(q_ref, kn_ref, kr_ref, v_ref, o_ref, kh_ref, vt_ref, s0_ref, s1_ref, m0_ref, m1_ref,
                   *, n_items, tiles):
    g = pl.program_id(0)
    nope = kn_ref.shape[1]
    t_cur = lax.rem(jnp.minimum(g, n_items - 1), tiles)
    s_refs = (s0_ref, s1_ref)
    m_refs = (m0_ref, m1_ref)

    @pl.when(g == 0)
    def _():
        s1_ref[...] = jnp.zeros(s1_ref.shape, s1_ref.dtype)
        m1_ref[...] = jnp.zeros(m1_ref.shape, m1_ref.dtype)
        vt_ref[...] = jnp.zeros(vt_ref.shape, vt_ref.dtype)

    @pl.when((t_cur == 0) & (g < n_items))
    def _():
        kh_ref[:, :nope] = kn_ref[...]
        kh_ref[:, nope:] = kr_ref[...]

    @pl.when((t_cur == 1) & (g < n_items))
    def _():
        vt_ref[...] = v_ref[...].astype(F32).T.astype(BF16)

    def body(parity):
        st, mx = _scores_t(kh_ref[...], q_ref[...])
        s_refs[parity][...] = st
        m_refs[parity][...] = mx
        o_ref[...] = _softmax_pv_t(s_refs[1 - parity][...], m_refs[1 - parity][...],
                                   vt_ref[...]).astype(o_ref.dtype)

    for parity in (0, 1):
        pl.when(lax.rem(g, 2) == parity)(functools.partial(body, parity))


def _attn_ctx_kernel(q_ref, kn_ref, kr_ref, v_ref, prev_ref, o_ref):
    del prev_ref
    k = jnp.concatenate([kn_ref[...], kr_ref[...]], axis=1)
    st, mx = _scores_t(k, q_ref[...])
    vt = v_ref[...].astype(F32).T.astype(BF16)
    o_ref[...] = _softmax_pv_t(st, mx, vt).astype(o_ref.dtype)


def _attention(q, kv, kr, batch, heads, ctx, hw, nope, dv):
    m = q.shape[0]
    s_len = m // batch
    seq = s_len - ctx
    tq = ATTN_TQ
    assert seq % tq == 0 and seq % ctx == 0 and seq // tq >= 2
    tiles = seq // tq
    n_items = batch * heads * tiles
    v_col0 = heads * nope // dv
    per_batch = lambda a: a.reshape(batch, s_len, a.shape[1])
    q3, kv3, kr3 = per_batch(q), per_batch(kv), per_batch(kr)

    def item(g):
        t = lax.rem(g, tiles)
        bh = g // tiles
        return bh // heads, lax.rem(bh, heads), t

    def cur(g):
        return item(jnp.minimum(g, n_items - 1))

    def prev(g):
        return item(jnp.maximum(g - 1, 0))

    out = pl.pallas_call(
        functools.partial(_attn_x_kernel, n_items=n_items, tiles=tiles),
        grid=(n_items + 1,),
        in_specs=[
            pl.BlockSpec((None, tq, hw), lambda g: (cur(g)[0], cur(g)[2], cur(g)[1])),
            pl.BlockSpec((None, s_len, nope), lambda g: (cur(g)[0], 0, cur(g)[1])),
            pl.BlockSpec((None, s_len, hw - nope), lambda g: (cur(g)[0], 0, 0)),
            pl.BlockSpec((None, s_len, dv), lambda g: (cur(g)[0], 0, v_col0 + cur(g)[1])),
        ],
        out_specs=pl.BlockSpec((None, tq, dv), lambda g: (prev(g)[0], prev(g)[2], prev(g)[1])),
        out_shape=jax.ShapeDtypeStruct((batch, s_len, heads * dv), BF16),
        scratch_shapes=[pltpu.VMEM((s_len, hw), BF16), pltpu.VMEM((dv, s_len), BF16),
                        pltpu.VMEM((s_len, tq), F32), pltpu.VMEM((s_len, tq), F32),
                        pltpu.VMEM((1, tq), F32), pltpu.VMEM((1, tq), F32)],
        compiler_params=_params(("arbitrary",),
                                _vmem_limit(tq * hw * 2 + s_len * (hw + dv) * 2 + tq * dv * 2,
                                            s_len * (hw + dv) * 2 + 2 * s_len * tq * 4, tq * s_len * 8)),
    )(q3, kv3, kr3, kv3)

    cb = seq // ctx
    out = pl.pallas_call(
        _attn_ctx_kernel,
        grid=(batch, heads),
        in_specs=[
            pl.BlockSpec((None, ctx, hw), lambda b, h: (b, cb, h)),
            pl.BlockSpec((None, ctx, nope), lambda b, h: (b, cb, h)),
            pl.BlockSpec((None, ctx, hw - nope), lambda b, h: (b, cb, 0)),
            pl.BlockSpec((None, ctx, dv), lambda b, h: (b, cb, v_col0 + h)),
            pl.BlockSpec(memory_space=pl.ANY),
        ],
        out_specs=pl.BlockSpec((None, ctx, dv), lambda b, h: (b, cb, h)),
        out_shape=jax.ShapeDtypeStruct((batch, s_len, heads * dv), BF16),
        input_output_aliases={4: 0},
        compiler_params=_params(("arbitrary", "arbitrary"), _vmem_limit(ctx * (2 * hw + 2 * dv) * 2, 0, 4 << 20)),
    )(q3, kv3, kr3, kv3, out)
    return out.reshape(m, heads * dv)


def _ret_qk_kernel(a_ref, w_ref, cos_ref, sin_ref, o_ref, *, dk, scale):
    acc = jnp.dot(a_ref[...], w_ref[...], preferred_element_type=F32)
    cos = cos_ref[...]
    sin = sin_ref[...]
    half = dk // 2
    for hh in range(acc.shape[1] // dk):
        x1 = acc[:, hh * dk:hh * dk + half]
        x2 = acc[:, hh * dk + half:(hh + 1) * dk]
        o_ref[:, hh * dk:hh * dk + half] = ((x1 * cos - x2 * sin) * scale).astype(o_ref.dtype)
        o_ref[:, hh * dk + half:(hh + 1) * dk] = ((x2 * cos + x1 * sin) * scale).astype(o_ref.dtype)


def _ret_qk(h, w, cos, sin, dk, scale, out_dtype):
    m, k = h.shape
    w_arr, _, n = _weight(w)
    half = dk // 2
    osz = jnp.dtype(out_dtype).itemsize
    need = lambda tm, tn: (tm * k * 2 + k * tn * 2 + tm * half * 8 + tm * tn * osz, 0, tm * tn * 12)
    tm, tn = _fit_tiles(m, n, ROW_TILES, tuple(t for t in COL_TILES if t % dk == 0), need)
    return pl.pallas_call(
        functools.partial(_ret_qk_kernel, dk=dk, scale=scale),
        grid=(m // tm, n // tn),
        in_specs=[
            pl.BlockSpec((tm, k), lambda i, j: (i, 0)),
            _weight_spec(w, tn),
            pl.BlockSpec((tm, half), lambda i, j: (i, 0)),
            pl.BlockSpec((tm, half), lambda i, j: (i, 0)),
        ],
        out_specs=pl.BlockSpec((tm, tn), lambda i, j: (i, j)),
        out_shape=jax.ShapeDtypeStruct((m, n), out_dtype),
        compiler_params=_params(("arbitrary", "arbitrary"), _vmem_limit(*need(tm, tn))),
    )(h, w_arr, cos, sin)


def _head_norm(y):
    d = y - jnp.mean(y, axis=-1, keepdims=True)
    return d * lax.rsqrt(jnp.mean(d * d, axis=-1, keepdims=True) + GN_EPS)


def _ret_chunk_kernel(cd_ref, q_ref, k_ref, v_ref, intra_ref, xi_ref, zeta_ref, g_ref, *rest, heads):
    if len(rest) == 3:
        prev_ref, y_ref, state_ref = rest
    else:
        prev_ref = None
        y_ref, state_ref = rest
    t = pl.program_id(1)

    @pl.when(t == 0)
    def _():
        state_ref[...] = jnp.zeros_like(state_ref)

    dk = q_ref.shape[1] // heads
    dv = v_ref.shape[1] // heads
    for h in range(heads):
        q = q_ref[:, h * dk:(h + 1) * dk]
        kf = k_ref[:, h * dk:(h + 1) * dk]
        v = v_ref[:, h * dv:(h + 1) * dv]
        st = state_ref[h]
        s = lax.dot_general(q, kf.astype(BF16), (((1,), (1,)), ((), ())),
                            preferred_element_type=F32) * intra_ref[h]
        y = jnp.dot(s.astype(BF16), v, preferred_element_type=F32)
        y = y + jnp.dot(q, st.astype(BF16), preferred_element_type=F32) * jnp.tile(xi_ref[h], (1, dv // LANES))
        cols = slice(h * dv, (h + 1) * dv)
        out = g_ref[:, cols].astype(F32) * _head_norm(y)
        if prev_ref is not None:
            out = prev_ref[:, cols].astype(F32) + out
        y_ref[:, cols] = out.astype(y_ref.dtype)
        kz = (kf * jnp.tile(zeta_ref[h], (1, dk // LANES))).astype(BF16)
        upd = lax.dot_general(kz, v, (((0,), (0,)), ((), ())), preferred_element_type=F32)
        state_ref[h] = st * cd_ref[h] + upd


def _ret_chunks(q, k, v, gate, prev, tables, batch, heads, ctx, backward):
    m = q.shape[0]
    c = SCAN_CHUNK
    assert ctx % c == 0 and (m // batch) % c == 0
    nch = m // batch // c
    ncc = ctx // c
    cd, intra, xi, zeta = tables
    dkh = q.shape[1]
    dvh = v.shape[1]

    nxc = nch - ncc
    if backward:
        def chunk(t):
            return nch - 1 - t
    else:
        def chunk(t):
            return jnp.where(t < ncc, nxc + t, t - ncc)

    row = lambda b, t: (b * nch + chunk(t), 0)
    const3 = lambda b, t: (0, 0, 0)
    extra = [gate] if prev is None else [gate, prev]
    return pl.pallas_call(
        functools.partial(_ret_chunk_kernel, heads=heads),
        grid=(batch, nch),
        in_specs=[
            pl.BlockSpec(memory_space=pltpu.SMEM),
            pl.BlockSpec((c, dkh), row),
            pl.BlockSpec((c, dkh), row),
            pl.BlockSpec((c, dvh), row),
            pl.BlockSpec(intra.shape, const3),
            pl.BlockSpec(xi.shape, const3),
            pl.BlockSpec(zeta.shape, const3),
        ] + [pl.BlockSpec((c, dvh), row)] * len(extra),
        out_specs=pl.BlockSpec((c, dvh), row),
        out_shape=jax.ShapeDtypeStruct((m, dvh), BF16),
        scratch_shapes=[pltpu.VMEM((heads, dkh // heads, dvh // heads), F32)],
        compiler_params=_params(("arbitrary", "arbitrary"),
                                _vmem_limit(c * dkh * 6 + c * dvh * 12 + (intra.size + xi.size + zeta.size) * 4,
                                            dkh * dvh // heads * 4, 8 << 20)),
    )(cd, q, k, v, intra, xi, zeta, *extra)


def _ret_tables(decay, backward):
    c = SCAN_CHUNK
    lg = -jnp.exp(decay.astype(F32))
    pos = jnp.arange(c, dtype=F32)
    diff = pos[:, None] - pos[None, :]
    lower = diff >= 0
    intra = jnp.where(lower, jnp.exp(lg[:, None, None] * jnp.where(lower, diff, 0.0)), 0.0)
    xi = jnp.exp(lg[:, None] * (pos + 1.0))
    zeta = jnp.exp(lg[:, None] * (c - 1.0 - pos))
    cd = jnp.exp(lg * c)
    if backward:
        intra = jnp.flip(intra, axis=(1, 2))
        xi = jnp.flip(xi, axis=1)
        zeta = jnp.flip(zeta, axis=1)
    heads = decay.shape[0]
    xi = jnp.broadcast_to(xi[:, :, None], (heads, c, LANES))
    zeta = jnp.broadcast_to(zeta[:, :, None], (heads, c, LANES))
    return cd, intra, xi, zeta


def _ffn_gu_kernel(a_ref, ap_ref, an_ref, wg_ref, wu_ref, cw_ref, cb_ref, o_ref, aext_ref, *, s_len, sub):
    i = pl.program_id(0)
    tm = a_ref.shape[0]
    halo = ap_ref.shape[0]

    @pl.when(pl.program_id(1) == 0)
    def _():
        aext_ref[:halo, :] = ap_ref[...]
        aext_ref[halo:halo + tm, :] = a_ref[...]
        aext_ref[halo + tm:, :] = an_ref[...]

    g_ext = jnp.dot(aext_ref[...], wg_ref[...], preferred_element_type=F32)
    u = jnp.dot(aext_ref[halo:halo + tm, :], wu_ref[...], preferred_element_type=F32)
    rows = g_ext.shape[0]
    g_prev = pltpu.roll(g_ext, 1, 0)[halo:halo + tm]
    g_next = pltpu.roll(g_ext, rows - 1, 0)[halo:halo + tm]
    g_cur = g_ext[halo:halo + tm]

    r = (i * tm + lax.broadcasted_iota(jnp.int32, (tm, 1), 0)).astype(F32)
    rb = r - jnp.floor((r + 0.5) * (1.0 / s_len)) * s_len
    seq = s_len - sub
    g_prev = jnp.where((rb == 0.0) | (rb == float(seq)), 0.0, g_prev)
    g_next = jnp.where((rb == float(seq - 1)) | (rb == float(s_len - 1)), 0.0, g_next)
    g = g_prev * cw_ref[0:1, :] + g_cur * cw_ref[1:2, :] + g_next * cw_ref[2:3, :] + cb_ref[...]
    o_ref[...] = (_silu(g) * u).astype(o_ref.dtype)


def _ffn_gate_up(h, w_gate, w_up, conv_w, conv_b, s_len, sub):
    m, k = h.shape
    wg_arr, _, n = _weight(w_gate)
    wu_arr = _weight(w_up)[0]
    halo = BF16_SUBLANES
    nhb = m // halo

    def need(tm, tn):
        ext = (tm + 2 * halo) * k * 2
        return (ext + 2 * k * tn * 2 + tm * tn * 2, ext, tm * tn * 20)

    tm, tn = _fit_tiles(m, n, tuple(t for t in ROW_TILES if t <= 1024),
                        (2 * MXU_COL,) + tuple(t for t in COL_TILES if t < 2 * MXU_COL), need)
    return pl.pallas_call(
        functools.partial(_ffn_gu_kernel, s_len=s_len, sub=sub),
        grid=(m // tm, n // tn),
        in_specs=[
            pl.BlockSpec((tm, k), lambda i, j: (i, 0)),
            pl.BlockSpec((halo, k), lambda i, j: (jnp.maximum(i * (tm // halo) - 1, 0), 0)),
            pl.BlockSpec((halo, k), lambda i, j: (jnp.minimum((i + 1) * (tm // halo), nhb - 1), 0)),
            _weight_spec(w_gate, tn),
            _weight_spec(w_up, tn),
            pl.BlockSpec((CONV_W, tn), lambda i, j: (0, j)),
            pl.BlockSpec((1, tn), lambda i, j: (0, j)),
        ],
        out_specs=pl.BlockSpec((tm, tn), lambda i, j: (i, j)),
        out_shape=jax.ShapeDtypeStruct((m, n), BF16),
        scratch_shapes=[pltpu.VMEM((tm + 2 * halo, k), BF16)],
        compiler_params=_params(("arbitrary", "arbitrary"), _vmem_limit(*need(tm, tn))),
    )(h, h, h, wg_arr, wu_arr, conv_w, conv_b.reshape(1, n))


def _expand_rows(tab_x, ctx_fill, batch, ctx):
    n = tab_x.shape[1]
    full = jnp.concatenate([tab_x, jnp.broadcast_to(ctx_fill, (ctx, n)).astype(F32)], axis=0)
    return jnp.tile(full, (batch, 1))


def _mla_rope_tables(seq, batch, ctx):
    rows = seq // GRID_W
    row = jnp.repeat(jnp.arange(rows), GRID_W).astype(F32)
    col = jnp.tile(jnp.arange(GRID_W), rows).astype(F32)
    axis_dim = MLA_ROPE // 2
    m = axis_dim // 2
    inv_axis = ROPE_BASE ** (-jnp.arange(0, axis_dim, 2, dtype=F32) / axis_dim)
    cos_parts, sin_parts = [], []
    for ang in (row[:, None] * inv_axis[None], col[:, None] * inv_axis[None]):
        c, s = jnp.cos(ang), jnp.sin(ang)
        cos_parts += [c, c]
        sin_parts += [-s, s]
    zeros = jnp.zeros((seq, MLA_ROPE), F32)
    cos = jnp.concatenate(cos_parts + [zeros], axis=1)
    sin = jnp.concatenate(sin_parts + [zeros], axis=1)
    fill_cos = jnp.concatenate([jnp.ones((MLA_ROPE,), F32), jnp.zeros((MLA_ROPE,), F32)])
    cos = _expand_rows(cos, fill_cos, batch, ctx)
    sin = _expand_rows(sin, jnp.zeros((2 * MLA_ROPE,), F32), batch, ctx)
    partner = []
    for base in (0, axis_dim):
        partner += [base + m + d for d in range(m)] + [base + d for d in range(m)]
    return cos, sin, jnp.array(partner, jnp.int32)


def _ret_rope_tables(seq, dk, batch, ctx):
    inv_ret = ROPE_BASE ** (-jnp.arange(0, dk, 2, dtype=F32) / dk)
    ang = jnp.arange(seq, dtype=F32)[:, None] * inv_ret[None]
    cos = _expand_rows(jnp.cos(ang), jnp.ones((dk // 2,), F32), batch, ctx)
    sin = _expand_rows(jnp.sin(ang), jnp.zeros((dk // 2,), F32), batch, ctx)
    return cos, sin


def _sub_table(vec_x, vec_c, nsb):
    b, d = vec_x.shape
    full = jnp.concatenate([jnp.broadcast_to(vec_x[:, None, :], (b, nsb - 1, d)),
                            jnp.broadcast_to(vec_c[None, None, :], (b, 1, d))], axis=1)
    return full.reshape(b * nsb, 1, d)


def _mla_layer(h, xc, gate, norm, tabs, batch, ctx, w_dq, q_g, w_uq, w_dkv, kv_g, w_ukv, w_o):
    cos, sin, partner = tabs
    rq = w_dq.shape[1]
    rkv = kv_g.shape[0]
    heads, nope, rope, dv = MLA_HEADS, MLA_NOPE, MLA_ROPE, MLA_V
    hw = nope + 2 * rope
    w_kr = w_dkv[:, rkv:]
    w_cat = jnp.concatenate([w_dq, w_dkv[:, :rkv], w_kr, w_kr[:, partner]], axis=1).astype(BF16)
    cq, ckv, kr = _mla_down(h, w_cat, q_g, kv_g, cos, sin, rq, rkv)

    scale = float(nope + rope) ** -0.5 * 1.4426950408889634
    wq = w_uq.reshape(rq, heads, nope + rope) * scale
    wq = jnp.concatenate([wq, wq[:, :, nope:][:, :, partner]], axis=2).reshape(rq, heads * hw).astype(BF16)
    q = _mla_q(cq, wq, cos, sin, hw, nope)

    wkv = w_ukv.reshape(rkv, heads, nope + dv)
    wkv = jnp.concatenate([wkv[:, :, :nope].reshape(rkv, heads * nope),
                           wkv[:, :, nope:].reshape(rkv, heads * dv)], axis=1).astype(BF16)
    kv = _matmul(ckv, wkv, BF16)

    o = _attention(q, kv, kr, batch, heads, ctx, hw, nope, dv)
    return _matmul_gated_residual(o, w_o, xc, gate, ctx, norm)


def _ret_layer(h, xc, gate, norm, tabs, batch, ctx, w_q, w_k, w_v, w_gf, w_gb, w_o, decay_f, decay_b):
    cos, sin = tabs
    heads = RET_HEADS
    dk = _weight(w_q)[2] // heads
    q = _ret_qk(h, w_q, cos, sin, dk, 1.0, BF16)
    k = _ret_qk(h, w_k, cos, sin, dk, float(dk) ** -0.5, F32)
    v = _matmul(h, w_v, BF16)
    gf = _matmul(h, w_gf, BF16, act="silu")
    gb = _matmul(h, w_gb, BF16, act="silu")
    fwd = _ret_chunks(q, k, v, gf, None, _ret_tables(decay_f, False), batch, heads, ctx, False)
    mix = _ret_chunks(q, k, v, gb, fwd, _ret_tables(decay_b, True), batch, heads, ctx, True)
    return _matmul_gated_residual(mix, w_o, xc, gate, ctx, norm)


def kernel(x, c, ctx, c_ctx, mod_w, mod_b, norm_mix_g, norm_ffn_g, mla_w_dq, mla_q_norm_g, mla_w_uq, mla_w_dkv, mla_kv_norm_g, mla_w_ukv, mla_w_o, ret_w_q, ret_w_k, ret_w_v, ret_w_gf, ret_w_gb, ret_w_o, ret_decay_f, ret_decay_b, ffn_w_gate, ffn_w_up, ffn_conv_w, ffn_conv_b, ffn_w_down, final_norm_g):
    batch, seq, d = x.shape
    ctx_len = ctx.shape[1]
    depth = mod_w.shape[0]
    s_len = ctx_len + seq
    nsb = s_len // ctx_len
    assert seq % ctx_len == 0 and ctx_len % RET_CHUNK == 0 and seq % GRID_W == 0
    assert mod_w.shape[2] == N_MOD * d

    cond_rows = 2 * 8
    cond = jnp.zeros((cond_rows, d), F32).at[:batch].set(c).at[batch].set(c_ctx)
    assert batch + 1 <= cond_rows
    mod = _modulation(cond, mod_w, mod_b)

    mla_tabs = _mla_rope_tables(seq, batch, ctx_len)
    ret_tabs = _ret_rope_tables(seq, ret_w_q.shape[2] // RET_HEADS, batch, ctx_len)

    parts = [[_sub_table(mod[l, :batch, k * d:(k + 1) * d], mod[l, batch, k * d:(k + 1) * d], nsb)
              for k in range(N_MOD)] for l in range(depth)]

    bf = lambda w: w.astype(BF16)
    mla_wo, ret_wq, ret_wk, ret_wv = bf(mla_w_o), bf(ret_w_q), bf(ret_w_k), bf(ret_w_v)
    ret_wgf, ret_wgb, ret_wo = bf(ret_w_gf), bf(ret_w_gb), bf(ret_w_o)
    ffn_wg, ffn_wu, ffn_wd = bf(ffn_w_gate), bf(ffn_w_up), bf(ffn_w_down)

    xc, h = _embed_norm(x, ctx, norm_mix_g[0], parts[0][1], parts[0][0])
    for l in range(depth):
        i = l // N_MIXERS
        _, _, g1, sh2, sc2, g2 = parts[l]
        ffn_norm = (norm_ffn_g[l], sc2, sh2)
        if l % N_MIXERS == 0:
            xc, h2 = _mla_layer(h, xc, g1, ffn_norm, mla_tabs, batch, ctx_len, mla_w_dq[i], mla_q_norm_g[i],
                                mla_w_uq[i], mla_w_dkv[i], mla_kv_norm_g[i], mla_w_ukv[i], (mla_wo, i))
        else:
            xc, h2 = _ret_layer(h, xc, g1, ffn_norm, ret_tabs, batch, ctx_len, (ret_wq, i), (ret_wk, i),
                                (ret_wv, i), (ret_wgf, i), (ret_wgb, i), (ret_wo, i),
                                ret_decay_f[i], ret_decay_b[i])
        act = _ffn_gate_up(h2, (ffn_wg, l), (ffn_wu, l), ffn_conv_w[l], ffn_conv_b[l], s_len, ctx_len)
        w_down = (ffn_wd, l)
        if l + 1 < depth:
            xc, h = _matmul_gated_residual(act, w_down, xc, g2, ctx_len,
                                           (norm_mix_g[l + 1], parts[l + 1][1], parts[l + 1][0]))
        else:
            xc = _matmul_gated_residual(act, w_down, xc, g2, ctx_len)

    return _final_norm(xc, final_norm_g, batch, seq, ctx_len).reshape(batch, seq, d)
```

```python
import functools

import jax
import jax.numpy as jnp
from jax import lax
from jax.experimental import pallas as pl
from jax.experimental.pallas import tpu as pltpu

GRID_W = 64
N_MIXERS = 2
MLA_HEADS = 16
MLA_NOPE = 128
MLA_ROPE = 64
MLA_V = 128
RET_HEADS = 8
RET_CHUNK = 128
CONV_W = 3
N_MOD = 6
ROPE_BASE = 10000.0
NORM_EPS = 1e-6
GN_EPS = 1e-5

V7X_VMEM_BYTES = 64 * 1024 * 1024
LANES = 128
BF16_SUBLANES = 16
MXU_COL = 256
VREG_ELEMS = 8 * LANES
RED_VREGS = 16
ATTN_TQ = 512
SCAN_CHUNK = MXU_COL
FFN_MAX_ROWS = 1024

F32 = jnp.float32
BF16 = jnp.bfloat16


VMEM_CAP = V7X_VMEM_BYTES - (6 << 20)
ROW_TILES = (2048, 1024, 512, 256)
COL_TILES = (1024, 512, 256, 128)


def _vmem_need(block_bytes, scratch_bytes=0, temp_bytes=0):
    return 2 * block_bytes + scratch_bytes + temp_bytes + (4 << 20)


def _vmem_limit(block_bytes, scratch_bytes=0, temp_bytes=0):
    return int(min(_vmem_need(block_bytes, scratch_bytes, temp_bytes), VMEM_CAP))


def _fit_tiles(m, n, tms, tns, need):
    for tm in tms:
        for tn in tns:
            if m % tm == 0 and n % tn == 0 and _vmem_need(*need(tm, tn)) <= VMEM_CAP:
                return tm, tn
    raise ValueError(f"no tile of {tms} x {tns} fits ({m}, {n})")


def _params(sem, limit, flags=None):
    return pltpu.CompilerParams(dimension_semantics=sem, vmem_limit_bytes=limit, flags=flags)


def _pick_tile(n, prefs):
    for t in prefs:
        if n % t == 0:
            return t
    raise ValueError(f"no tile in {prefs} divides {n}")


def _silu(x):
    return x * (0.5 * jnp.tanh(0.5 * x) + 0.5)


def _mod_kernel(a_ref, w_ref, b_ref, o_ref):
    a = _silu(a_ref[...]).astype(BF16)
    acc = jnp.dot(a, w_ref[0].astype(BF16), preferred_element_type=F32)
    o_ref[0] = acc + b_ref[0]


def _modulation(cond, mod_w, mod_b):
    depth, d, n = mod_w.shape
    rows = cond.shape[0]
    tn = _pick_tile(n, (1024, 512, 256, 128))
    return pl.pallas_call(
        _mod_kernel,
        grid=(depth, n // tn),
        in_specs=[
            pl.BlockSpec((rows, d), lambda l, j: (0, 0)),
            pl.BlockSpec((1, d, tn), lambda l, j: (l, 0, j)),
            pl.BlockSpec((1, 1, tn), lambda l, j: (l, 0, j)),
        ],
        out_specs=pl.BlockSpec((1, rows, tn), lambda l, j: (l, 0, j)),
        out_shape=jax.ShapeDtypeStruct((depth, rows, n), F32),
        compiler_params=_params(("arbitrary", "arbitrary"),
                                _vmem_limit(d * tn * 4 + rows * (d + tn) * 4, 0, d * tn * 2)),
    )(cond, mod_w, mod_b.reshape(depth, 1, n))


def _embed_norm_kernel(x_ref, c_ref, g_ref, sc_ref, sh_ref, xc_ref, h_ref, *, nx):
    x = jnp.where(pl.program_id(1) < nx, x_ref[...], c_ref[...])
    xc_ref[...] = x
    y = x * lax.rsqrt(jnp.mean(x * x, axis=-1, keepdims=True) + NORM_EPS) * g_ref[...]
    h_ref[...] = (y * (1.0 + sc_ref[0]) + sh_ref[0]).astype(h_ref.dtype)


def _embed_norm(x, ctx, g, sc, sh):
    batch, seq, d = x.shape
    sub = ctx.shape[1]
    nx = seq // sub
    nsb = nx + 1
    m = batch * nsb * sub
    row = lambda b, j: (b * nsb + j, 0)
    tab = lambda b, j: (b * nsb + j, 0, 0)
    return pl.pallas_call(
        functools.partial(_embed_norm_kernel, nx=nx),
        grid=(batch, nsb),
        in_specs=[
            pl.BlockSpec((None, sub, d), lambda b, j: (b, jnp.minimum(j, nx - 1), 0)),
            pl.BlockSpec((None, sub, d), lambda b, j: (b, 0, 0)),
            pl.BlockSpec((1, d), lambda b, j: (0, 0)),
            pl.BlockSpec((1, 1, d), tab),
            pl.BlockSpec((1, 1, d), tab),
        ],
        out_specs=[pl.BlockSpec((sub, d), row), pl.BlockSpec((sub, d), row)],
        out_shape=[jax.ShapeDtypeStruct((m, d), F32), jax.ShapeDtypeStruct((m, d), BF16)],
        compiler_params=_params(("arbitrary", "arbitrary"), _vmem_limit(sub * d * 14, 0, sub * d * 8)),
    )(x, ctx, g.reshape(1, d), sc, sh)


def _final_norm_kernel(x_ref, g_ref, o_ref):
    x = x_ref[...]
    o_ref[...] = x * lax.rsqrt(jnp.mean(x * x, axis=-1, keepdims=True) + NORM_EPS) * g_ref[...]


def _final_norm(xc, g, batch, seq, sub):
    m, d = xc.shape
    nsb = m // batch // sub
    nx = seq // sub
    return pl.pallas_call(
        _final_norm_kernel,
        grid=(batch, nx),
        in_specs=[
            pl.BlockSpec((sub, d), lambda b, j: (b * nsb + j, 0)),
            pl.BlockSpec((1, d), lambda b, j: (0, 0)),
        ],
        out_specs=pl.BlockSpec((sub, d), lambda b, j: (b * nx + j, 0)),
        out_shape=jax.ShapeDtypeStruct((batch * seq, d), F32),
        compiler_params=_params(("arbitrary", "arbitrary"), _vmem_limit(sub * d * 8, 0, sub * d * 8)),
    )(xc, g.reshape(1, d))


def _mm_kernel(a_ref, w_ref, o_ref, *, silu):
    acc = jnp.dot(a_ref[...], w_ref[...], preferred_element_type=F32)
    o_ref[...] = (_silu(acc) if silu else acc).astype(o_ref.dtype)


def _weight(w):
    arr = w[0] if isinstance(w, tuple) else w
    return arr, arr.shape[-2], arr.shape[-1]


def _weight_spec(w, tn):
    if isinstance(w, tuple):
        arr, layer = w
        return pl.BlockSpec((None, arr.shape[1], tn), lambda i, j: (layer, 0, j))
    return pl.BlockSpec((w.shape[0], tn), lambda i, j: (0, j))


def _matmul(a, w, out_dtype, silu=False):
    m, k = a.shape
    w_arr, _, n = _weight(w)
    osz = jnp.dtype(out_dtype).itemsize
    need = lambda tm, tn: (tm * k * 2 + k * tn * 2 + tm * tn * osz, 0, tm * tn * 8)
    tm, tn = _fit_tiles(m, n, ROW_TILES, COL_TILES, need)
    return pl.pallas_call(
        functools.partial(_mm_kernel, silu=silu),
        grid=(m // tm, n // tn),
        in_specs=[
            pl.BlockSpec((tm, k), lambda i, j: (i, 0)),
            _weight_spec(w, tn),
        ],
        out_specs=pl.BlockSpec((tm, tn), lambda i, j: (i, j)),
        out_shape=jax.ShapeDtypeStruct((m, n), out_dtype),
        compiler_params=_params(("arbitrary", "arbitrary"), _vmem_limit(*need(tm, tn))),
    )(a, w_arr)


def _mm_res_kernel(a_ref, w_ref, res_ref, gate_ref, *rest, sub):
    fused = len(rest) > 1
    if fused:
        g_ref, sc_ref, sh_ref, x_ref, h_ref, row_ref = rest
    else:
        (x_ref,) = rest
    j = pl.program_id(1)
    nsub = a_ref.shape[0] // sub
    tn = w_ref.shape[1]
    acc = jnp.dot(a_ref[...], w_ref[...], preferred_element_type=F32)
    for s in range(nsub):
        rows = slice(s * sub, (s + 1) * sub)
        xs = res_ref[rows, :] + gate_ref[s] * acc[rows, :]
        x_ref[rows, :] = xs
        if fused:
            row_ref[j, rows, :] = xs
    if not fused:
        return
    nj = row_ref.shape[0]

    @pl.when(j == nj - 1)
    def _():
        ss = jnp.sum(row_ref[0] * row_ref[0], axis=-1, keepdims=True)
        for jj in range(1, nj):
            ss = ss + jnp.sum(row_ref[jj] * row_ref[jj], axis=-1, keepdims=True)
        inv = lax.rsqrt(ss * (1.0 / (nj * tn)) + NORM_EPS)
        for jj in range(nj):
            cols = slice(jj * tn, (jj + 1) * tn)
            for s in range(nsub):
                rows = slice(s * sub, (s + 1) * sub)
                y = row_ref[jj, rows, :] * inv[rows, :] * g_ref[:, cols]
                h_ref[rows, cols] = (y * (1.0 + sc_ref[s][:, cols]) + sh_ref[s][:, cols]).astype(h_ref.dtype)


def _matmul_gated_residual(a, w, res, gate, sub, norm=None):
    m, k = a.shape
    w_arr, _, n = _weight(w)
    fused = norm is not None

    def need(tm, tn):
        blocks = tm * k * 2 + k * tn * 2 + tm * tn * 8
        return (blocks + tm * n * 2, tm * n * 4, tm * tn * 4) if fused else (blocks, 0, tm * tn * 4)

    tm, tn = _fit_tiles(m, n, tuple(t for t in ROW_TILES if t <= 512), (1024, 2 * MXU_COL), need)
    nsub = tm // sub
    in_specs = [
        pl.BlockSpec((tm, k), lambda i, j: (i, 0)),
        _weight_spec(w, tn),
        pl.BlockSpec((tm, tn), lambda i, j: (i, j)),
        pl.BlockSpec((nsub, 1, tn), lambda i, j: (i, 0, j)),
    ]
    x_spec = pl.BlockSpec((tm, tn), lambda i, j: (i, j))
    x_shape = jax.ShapeDtypeStruct((m, n), F32)
    params = _params(("arbitrary", "arbitrary"), _vmem_limit(*need(tm, tn)))
    if not fused:
        return pl.pallas_call(
            functools.partial(_mm_res_kernel, sub=sub),
            grid=(m // tm, n // tn),
            in_specs=in_specs,
            out_specs=x_spec,
            out_shape=x_shape,
            compiler_params=params,
        )(a, w_arr, res, gate)
    g, sc, sh = norm
    return pl.pallas_call(
        functools.partial(_mm_res_kernel, sub=sub),
        grid=(m // tm, n // tn),
        in_specs=in_specs + [
            pl.BlockSpec((1, n), lambda i, j: (0, 0)),
            pl.BlockSpec((nsub, 1, n), lambda i, j: (i, 0, 0)),
            pl.BlockSpec((nsub, 1, n), lambda i, j: (i, 0, 0)),
        ],
        out_specs=[x_spec, pl.BlockSpec((tm, n), lambda i, j: (i, 0))],
        out_shape=[x_shape, jax.ShapeDtypeStruct((m, n), BF16)],
        scratch_shapes=[pltpu.VMEM((n // tn, tm, tn), F32)],
        compiler_params=params,
    )(a, w_arr, res, gate, g.reshape(1, n), sc, sh)


def _rms(x, g):
    return x * lax.rsqrt(jnp.mean(x * x, axis=-1, keepdims=True) + NORM_EPS) * g


def _rope_pair(r, cos, sin):
    half = r.shape[-1] // 2
    return r * cos + pltpu.roll(r, half, 1) * sin


def _mla_down_kernel(a_ref, w_ref, qg_ref, kvg_ref, cos_ref, sin_ref, cq_ref, ckv_ref, kr_ref, *, rq, rkv):
    acc = jnp.dot(a_ref[...], w_ref[...], preferred_element_type=F32)
    cq_ref[...] = _rms(acc[:, :rq], qg_ref[...]).astype(cq_ref.dtype)
    ckv_ref[...] = _rms(acc[:, rq:rq + rkv], kvg_ref[...]).astype(ckv_ref.dtype)
    kr_ref[...] = _rope_pair(acc[:, rq + rkv:], cos_ref[...], sin_ref[...]).astype(kr_ref.dtype)


def _mla_down(h, w_cat, q_g, kv_g, cos, sin, rq, rkv):
    m, k = h.shape
    n = w_cat.shape[1]
    nr = n - rq - rkv
    tm = _pick_tile(m, (1024, 512, 256))
    return pl.pallas_call(
        functools.partial(_mla_down_kernel, rq=rq, rkv=rkv),
        grid=(m // tm,),
        in_specs=[
            pl.BlockSpec((tm, k), lambda i: (i, 0)),
            pl.BlockSpec((k, n), lambda i: (0, 0)),
            pl.BlockSpec((1, rq), lambda i: (0, 0)),
            pl.BlockSpec((1, rkv), lambda i: (0, 0)),
            pl.BlockSpec((tm, nr), lambda i: (i, 0)),
            pl.BlockSpec((tm, nr), lambda i: (i, 0)),
        ],
        out_specs=[
            pl.BlockSpec((tm, rq), lambda i: (i, 0)),
            pl.BlockSpec((tm, rkv), lambda i: (i, 0)),
            pl.BlockSpec((tm, nr), lambda i: (i, 0)),
        ],
        out_shape=[
            jax.ShapeDtypeStruct((m, rq), BF16),
            jax.ShapeDtypeStruct((m, rkv), BF16),
            jax.ShapeDtypeStruct((m, nr), BF16),
        ],
        compiler_params=_params(("arbitrary",),
                                _vmem_limit(tm * k * 2 + k * n * 2 + tm * nr * 8 + tm * n * 2, 0, tm * n * 8)),
    )(h, w_cat, q_g.reshape(1, rq), kv_g.reshape(1, rkv), cos, sin)


def _mla_q_kernel(a_ref, w_ref, cos_ref, sin_ref, o_ref, *, hw, nope):
    acc = jnp.dot(a_ref[...], w_ref[...], preferred_element_type=F32)
    cos = cos_ref[...]
    sin = sin_ref[...]
    for hh in range(acc.shape[1] // hw):
        o_ref[:, hh * hw:hh * hw + nope] = acc[:, hh * hw:hh * hw + nope].astype(o_ref.dtype)
        o_ref[:, hh * hw + nope:(hh + 1) * hw] = _rope_pair(
            acc[:, hh * hw + nope:(hh + 1) * hw], cos, sin).astype(o_ref.dtype)


def _mla_q(cq, w_uq, cos, sin, hw, nope):
    m, k = cq.shape
    n = w_uq.shape[1]
    tm = _pick_tile(m, (1024, 512, 256))
    tn = _pick_tile(n, (1024, 512, 256))
    nr = hw - nope
    return pl.pallas_call(
        functools.partial(_mla_q_kernel, hw=hw, nope=nope),
        grid=(m // tm, n // tn),
        in_specs=[
            pl.BlockSpec((tm, k), lambda i, j: (i, 0)),
            pl.BlockSpec((k, tn), lambda i, j: (0, j)),
            pl.BlockSpec((tm, nr), lambda i, j: (i, 0)),
            pl.BlockSpec((tm, nr), lambda i, j: (i, 0)),
        ],
        out_specs=pl.BlockSpec((tm, tn), lambda i, j: (i, j)),
        out_shape=jax.ShapeDtypeStruct((m, n), BF16),
        compiler_params=_params(("arbitrary", "arbitrary"),
                                _vmem_limit(tm * k * 2 + k * tn * 2 + tm * nr * 8 + tm * tn * 2, 0, tm * tn * 8)),
    )(cq, w_uq, cos, sin)


def _scores_t(k, q):
    nq, nk = q.shape[0], k.shape[0]
    st = lax.dot_general(k, q, (((1,), (1,)), ((), ())), preferred_element_type=F32)
    rows = RED_VREGS * VREG_ELEMS // nq
    mx = jnp.max(jnp.max(st.reshape(nk // rows, rows, nq), axis=0), axis=0, keepdims=True)
    return st, mx


def _softmax_pv_t(st, mx, vt):
    nk, nq = st.shape
    rows = RED_VREGS * VREG_ELEMS // nq
    p = jnp.exp2(st.reshape(nk // rows, rows, nq) - mx)
    den = jnp.sum(jnp.sum(p, axis=0), axis=0, keepdims=True)
    ot = jnp.dot(vt, p.reshape(nk, nq).astype(BF16), preferred_element_type=F32)
    return (ot / den).T


def _attn_x_kernel(q_ref, kn_ref, kr_ref, v_ref, o_ref, kh_ref, vt_ref, s0_ref, s1_ref, m0_ref, m1_ref,
                   *, n_items, tiles):
    g = pl.program_id(0)
    nope = kn_ref.shape[1]
    t_cur = lax.rem(jnp.minimum(g, n_items - 1), tiles)
    s_refs = (s0_ref, s1_ref)
    m_refs = (m0_ref, m1_ref)

    @pl.when(g == 0)
    def _():
        s1_ref[...] = jnp.zeros(s1_ref.shape, s1_ref.dtype)
        m1_ref[...] = jnp.zeros(m1_ref.shape, m1_ref.dtype)
        vt_ref[...] = jnp.zeros(vt_ref.shape, vt_ref.dtype)

    @pl.when((t_cur == 0) & (g < n_items))
    def _():
        kh_ref[:, :nope] = kn_ref[...]
        kh_ref[:, nope:] = kr_ref[...]

    @pl.when((t_cur == 1) & (g < n_items))
    def _():
        vt_ref[...] = v_ref[...].astype(F32).T.astype(BF16)

    def body(parity):
        st, mx = _scores_t(kh_ref[...], q_ref[...])
        s_refs[parity][...] = st
        m_refs[parity][...] = mx
        o_ref[...] = _softmax_pv_t(s_refs[1 - parity][...], m_refs[1 - parity][...],
                                   vt_ref[...]).astype(o_ref.dtype)

    for parity in (0, 1):
        pl.when(lax.rem(g, 2) == parity)(functools.partial(body, parity))


def _attn_ctx_kernel(q_ref, kn_ref, kr_ref, v_ref, prev_ref, o_ref, *, heads):
    del prev_ref
    hw = q_ref.shape[1] // heads
    nope = kn_ref.shape[1] // heads
    dv = v_ref.shape[1] // heads
    for h in range(heads):
        k = jnp.concatenate([kn_ref[:, h * nope:(h + 1) * nope], kr_ref[...]], axis=1)
        st, mx = _scores_t(k, q_ref[:, h * hw:(h + 1) * hw])
        vt = v_ref[:, h * dv:(h + 1) * dv].astype(F32).T.astype(BF16)
        o_ref[:, h * dv:(h + 1) * dv] = _softmax_pv_t(st, mx, vt).astype(o_ref.dtype)


def _attention(q, kv, kr, batch, heads, ctx, hw, nope, dv):
    m = q.shape[0]
    s_len = m // batch
    seq = s_len - ctx
    tq = ATTN_TQ
    assert seq % tq == 0 and seq % ctx == 0 and seq // tq >= 2
    tiles = seq // tq
    n_items = batch * heads * tiles
    v_col0 = heads * nope // dv
    per_batch = lambda a: a.reshape(batch, s_len, a.shape[1])
    q3, kv3, kr3 = per_batch(q), per_batch(kv), per_batch(kr)

    def item(g):
        t = lax.rem(g, tiles)
        bh = g // tiles
        return bh // heads, lax.rem(bh, heads), t

    def cur(g):
        return item(jnp.minimum(g, n_items - 1))

    def prev(g):
        return item(jnp.maximum(g - 1, 0))

    out = pl.pallas_call(
        functools.partial(_attn_x_kernel, n_items=n_items, tiles=tiles),
        grid=(n_items + 1,),
        in_specs=[
            pl.BlockSpec((None, tq, hw), lambda g: (cur(g)[0], cur(g)[2], cur(g)[1])),
            pl.BlockSpec((None, s_len, nope), lambda g: (cur(g)[0], 0, cur(g)[1])),
            pl.BlockSpec((None, s_len, hw - nope), lambda g: (cur(g)[0], 0, 0)),
            pl.BlockSpec((None, s_len, dv), lambda g: (cur(g)[0], 0, v_col0 + cur(g)[1])),
        ],
        out_specs=pl.BlockSpec((None, tq, dv), lambda g: (prev(g)[0], prev(g)[2], prev(g)[1])),
        out_shape=jax.ShapeDtypeStruct((batch, s_len, heads * dv), BF16),
        scratch_shapes=[pltpu.VMEM((s_len, hw), BF16), pltpu.VMEM((dv, s_len), BF16),
                        pltpu.VMEM((s_len, tq), F32), pltpu.VMEM((s_len, tq), F32),
                        pltpu.VMEM((1, tq), F32), pltpu.VMEM((1, tq), F32)],
        compiler_params=_params(("arbitrary",),
                                _vmem_limit(tq * hw * 2 + s_len * (hw + dv) * 2 + tq * dv * 2,
                                            s_len * (hw + dv) * 2 + 2 * s_len * tq * 4, tq * s_len * 8)),
    )(q3, kv3, kr3, kv3)

    cb = seq // ctx
    out = pl.pallas_call(
        functools.partial(_attn_ctx_kernel, heads=heads),
        grid=(batch,),
        in_specs=[
            pl.BlockSpec((None, ctx, heads * hw), lambda b: (b, cb, 0)),
            pl.BlockSpec((None, ctx, heads * nope), lambda b: (b, cb, 0)),
            pl.BlockSpec((None, ctx, hw - nope), lambda b: (b, cb, 0)),
            pl.BlockSpec((None, ctx, heads * dv), lambda b: (b, cb, v_col0 // heads)),
            pl.BlockSpec(memory_space=pl.ANY),
        ],
        out_specs=pl.BlockSpec((None, ctx, heads * dv), lambda b: (b, cb, 0)),
        out_shape=jax.ShapeDtypeStruct((batch, s_len, heads * dv), BF16),
        input_output_aliases={4: 0},
        compiler_params=_params(("arbitrary",), _vmem_limit(ctx * heads * (hw + nope + 2 * dv) * 2, 0, 8 << 20)),
    )(q3, kv3, kr3, kv3, out)
    return out.reshape(m, heads * dv)


def _ret_qk_kernel(a_ref, w_ref, cos_ref, sin_ref, o_ref, *, dk, scale):
    acc = jnp.dot(a_ref[...], w_ref[...], preferred_element_type=F32)
    cos = cos_ref[...]
    sin = sin_ref[...]
    half = dk // 2
    for hh in range(acc.shape[1] // dk):
        x1 = acc[:, hh * dk:hh * dk + half]
        x2 = acc[:, hh * dk + half:(hh + 1) * dk]
        o_ref[:, hh * dk:hh * dk + half] = ((x1 * cos - x2 * sin) * scale).astype(o_ref.dtype)
        o_ref[:, hh * dk + half:(hh + 1) * dk] = ((x2 * cos + x1 * sin) * scale).astype(o_ref.dtype)


def _ret_qk(h, w, cos, sin, dk, scale, out_dtype):
    m, k = h.shape
    w_arr, _, n = _weight(w)
    half = dk // 2
    osz = jnp.dtype(out_dtype).itemsize
    need = lambda tm, tn: (tm * k * 2 + k * tn * 2 + tm * half * 8 + tm * tn * osz, 0, tm * tn * 12)
    tm, tn = _fit_tiles(m, n, tuple(t for t in ROW_TILES if t <= 1024),
                        tuple(t for t in COL_TILES if t % dk == 0), need)
    return pl.pallas_call(
        functools.partial(_ret_qk_kernel, dk=dk, scale=scale),
        grid=(m // tm, n // tn),
        in_specs=[
            pl.BlockSpec((tm, k), lambda i, j: (i, 0)),
            _weight_spec(w, tn),
            pl.BlockSpec((tm, half), lambda i, j: (i, 0)),
            pl.BlockSpec((tm, half), lambda i, j: (i, 0)),
        ],
        out_specs=pl.BlockSpec((tm, tn), lambda i, j: (i, j)),
        out_shape=jax.ShapeDtypeStruct((m, n), out_dtype),
        compiler_params=_params(("arbitrary", "arbitrary"), _vmem_limit(*need(tm, tn))),
    )(h, w_arr, cos, sin)


def _head_norm(y):
    d = y - jnp.mean(y, axis=-1, keepdims=True)
    return d * lax.rsqrt(jnp.mean(d * d, axis=-1, keepdims=True) + GN_EPS)


def _ret_chunk_kernel(cd_ref, q_ref, k_ref, v_ref, intra_ref, xi_ref, zeta_ref, g_ref, *rest, heads):
    if len(rest) == 3:
        prev_ref, y_ref, state_ref = rest
    else:
        prev_ref = None
        y_ref, state_ref = rest
    t = pl.program_id(1)

    @pl.when(t == 0)
    def _():
        state_ref[...] = jnp.zeros_like(state_ref)

    dk = q_ref.shape[1] // heads
    dv = v_ref.shape[1] // heads
    for h in range(heads):
        q = q_ref[:, h * dk:(h + 1) * dk]
        kf = k_ref[:, h * dk:(h + 1) * dk]
        v = v_ref[:, h * dv:(h + 1) * dv]
        st = state_ref[h]
        s = lax.dot_general(q, kf.astype(BF16), (((1,), (1,)), ((), ())),
                            preferred_element_type=F32) * intra_ref[h]
        y = jnp.dot(s.astype(BF16), v, preferred_element_type=F32)
        y = y + jnp.dot(q, st.astype(BF16), preferred_element_type=F32) * jnp.tile(xi_ref[h], (1, dv // LANES))
        cols = slice(h * dv, (h + 1) * dv)
        out = g_ref[:, cols].astype(F32) * _head_norm(y)
        if prev_ref is not None:
            out = prev_ref[:, cols].astype(F32) + out
        y_ref[:, cols] = out.astype(y_ref.dtype)
        kz = (kf * jnp.tile(zeta_ref[h], (1, dk // LANES))).astype(BF16)
        upd = lax.dot_general(kz, v, (((0,), (0,)), ((), ())), preferred_element_type=F32)
        state_ref[h] = st * cd_ref[h] + upd


def _ret_chunks(q, k, v, gate, prev, tables, batch, heads, ctx, backward):
    m = q.shape[0]
    c = SCAN_CHUNK
    assert ctx % c == 0 and (m // batch) % c == 0
    nch = m // batch // c
    ncc = ctx // c
    cd, intra, xi, zeta = tables
    dkh = q.shape[1]
    dvh = v.shape[1]

    nxc = nch - ncc
    if backward:
        def chunk(t):
            return nch - 1 - t
    else:
        def chunk(t):
            return jnp.where(t < ncc, nxc + t, t - ncc)

    row = lambda b, t: (b * nch + chunk(t), 0)
    const3 = lambda b, t: (0, 0, 0)
    extra = [gate] if prev is None else [gate, prev]
    return pl.pallas_call(
        functools.partial(_ret_chunk_kernel, heads=heads),
        grid=(batch, nch),
        in_specs=[
            pl.BlockSpec(memory_space=pltpu.SMEM),
            pl.BlockSpec((c, dkh), row),
            pl.BlockSpec((c, dkh), row),
            pl.BlockSpec((c, dvh), row),
            pl.BlockSpec(intra.shape, const3),
            pl.BlockSpec(xi.shape, const3),
            pl.BlockSpec(zeta.shape, const3),
        ] + [pl.BlockSpec((c, dvh), row)] * len(extra),
        out_specs=pl.BlockSpec((c, dvh), row),
        out_shape=jax.ShapeDtypeStruct((m, dvh), BF16),
        scratch_shapes=[pltpu.VMEM((heads, dkh // heads, dvh // heads), F32)],
        compiler_params=_params(("arbitrary", "arbitrary"),
                                _vmem_limit(c * dkh * 6 + c * dvh * 12 + (intra.size + xi.size + zeta.size) * 4,
                                            dkh * dvh // heads * 4, 8 << 20)),
    )(cd, q, k, v, intra, xi, zeta, *extra)


def _ret_tables(decay, backward):
    c = SCAN_CHUNK
    lg = -jnp.exp(decay.astype(F32))
    pos = jnp.arange(c, dtype=F32)
    diff = pos[:, None] - pos[None, :]
    lower = diff >= 0
    intra = jnp.where(lower, jnp.exp(lg[:, None, None] * jnp.where(lower, diff, 0.0)), 0.0)
    xi = jnp.exp(lg[:, None] * (pos + 1.0))
    zeta = jnp.exp(lg[:, None] * (c - 1.0 - pos))
    cd = jnp.exp(lg * c)
    if backward:
        intra = jnp.flip(intra, axis=(1, 2))
        xi = jnp.flip(xi, axis=1)
        zeta = jnp.flip(zeta, axis=1)
    heads = decay.shape[0]
    xi = jnp.broadcast_to(xi[:, :, None], (heads, c, LANES))
    zeta = jnp.broadcast_to(zeta[:, :, None], (heads, c, LANES))
    return cd, intra, xi, zeta


def _ffn_gu_kernel(a_ref, ap_ref, an_ref, wg_ref, wu_ref, cw_ref, cb_ref, o_ref, aext_ref, *, s_len, sub):
    i = pl.program_id(0)
    tm = a_ref.shape[0]
    halo = ap_ref.shape[0]

    @pl.when(pl.program_id(1) == 0)
    def _():
        aext_ref[:halo, :] = ap_ref[...]
        aext_ref[halo:halo + tm, :] = a_ref[...]
        aext_ref[halo + tm:, :] = an_ref[...]

    g_ext = jnp.dot(aext_ref[...], wg_ref[...], preferred_element_type=F32)
    u = jnp.dot(aext_ref[halo:halo + tm, :], wu_ref[...], preferred_element_type=F32)
    rows = g_ext.shape[0]
    g_prev = pltpu.roll(g_ext, 1, 0)[halo:halo + tm]
    g_next = pltpu.roll(g_ext, rows - 1, 0)[halo:halo + tm]
    g_cur = g_ext[halo:halo + tm]

    r = (i * tm + lax.broadcasted_iota(jnp.int32, (tm, 1), 0)).astype(F32)
    rb = r - jnp.floor((r + 0.5) * (1.0 / s_len)) * s_len
    seq = s_len - sub
    g_prev = jnp.where((rb == 0.0) | (rb == float(seq)), 0.0, g_prev)
    g_next = jnp.where((rb == float(seq - 1)) | (rb == float(s_len - 1)), 0.0, g_next)
    g = g_prev * cw_ref[0:1, :] + g_cur * cw_ref[1:2, :] + g_next * cw_ref[2:3, :] + cb_ref[...]
    o_ref[...] = (_silu(g) * u).astype(o_ref.dtype)


def _ffn_gate_up(h, w_gate, w_up, conv_w, conv_b, s_len, sub):
    m, k = h.shape
    wg_arr, _, n = _weight(w_gate)
    wu_arr = _weight(w_up)[0]
    halo = BF16_SUBLANES
    nhb = m // halo

    def need(tm, tn):
        ext = (tm + 2 * halo) * k * 2
        return (ext + 2 * k * tn * 2 + tm * tn * 2, ext, tm * tn * 20)

    tm, tn = _fit_tiles(m, n, tuple(t for t in ROW_TILES if t <= FFN_MAX_ROWS),
                        (2 * MXU_COL,) + tuple(t for t in COL_TILES if t < 2 * MXU_COL), need)
    return pl.pallas_call(
        functools.partial(_ffn_gu_kernel, s_len=s_len, sub=sub),
        grid=(m // tm, n // tn),
        in_specs=[
            pl.BlockSpec((tm, k), lambda i, j: (i, 0)),
            pl.BlockSpec((halo, k), lambda i, j: (jnp.maximum(i * (tm // halo) - 1, 0), 0)),
            pl.BlockSpec((halo, k), lambda i, j: (jnp.minimum((i + 1) * (tm // halo), nhb - 1), 0)),
            _weight_spec(w_gate, tn),
            _weight_spec(w_up, tn),
            pl.BlockSpec((CONV_W, tn), lambda i, j: (0, j)),
            pl.BlockSpec((1, tn), lambda i, j: (0, j)),
        ],
        out_specs=pl.BlockSpec((tm, tn), lambda i, j: (i, j)),
        out_shape=jax.ShapeDtypeStruct((m, n), BF16),
        scratch_shapes=[pltpu.VMEM((tm + 2 * halo, k), BF16)],
        compiler_params=_params(("arbitrary", "arbitrary"), _vmem_limit(*need(tm, tn))),
    )(h, h, h, wg_arr, wu_arr, conv_w, conv_b.reshape(1, n))


def _expand_rows(tab_x, ctx_fill, batch, ctx):
    n = tab_x.shape[1]
    full = jnp.concatenate([tab_x, jnp.broadcast_to(ctx_fill, (ctx, n)).astype(F32)], axis=0)
    return jnp.tile(full, (batch, 1))


def _mla_rope_tables(seq, batch, ctx):
    rows = seq // GRID_W
    row = jnp.repeat(jnp.arange(rows), GRID_W).astype(F32)
    col = jnp.tile(jnp.arange(GRID_W), rows).astype(F32)
    axis_dim = MLA_ROPE // 2
    m = axis_dim // 2
    inv_axis = ROPE_BASE ** (-jnp.arange(0, axis_dim, 2, dtype=F32) / axis_dim)
    cos_parts, sin_parts = [], []
    for ang in (row[:, None] * inv_axis[None], col[:, None] * inv_axis[None]):
        c, s = jnp.cos(ang), jnp.sin(ang)
        cos_parts += [c, c]
        sin_parts += [-s, s]
    zeros = jnp.zeros((seq, MLA_ROPE), F32)
    cos = jnp.concatenate(cos_parts + [zeros], axis=1)
    sin = jnp.concatenate(sin_parts + [zeros], axis=1)
    fill_cos = jnp.concatenate([jnp.ones((MLA_ROPE,), F32), jnp.zeros((MLA_ROPE,), F32)])
    cos = _expand_rows(cos, fill_cos, batch, ctx)
    sin = _expand_rows(sin, jnp.zeros((2 * MLA_ROPE,), F32), batch, ctx)
    partner = []
    for base in (0, axis_dim):
        partner += [base + m + d for d in range(m)] + [base + d for d in range(m)]
    return cos, sin, jnp.array(partner, jnp.int32)


def _ret_rope_tables(seq, dk, batch, ctx):
    inv_ret = ROPE_BASE ** (-jnp.arange(0, dk, 2, dtype=F32) / dk)
    ang = jnp.arange(seq, dtype=F32)[:, None] * inv_ret[None]
    cos = _expand_rows(jnp.cos(ang), jnp.ones((dk // 2,), F32), batch, ctx)
    sin = _expand_rows(jnp.sin(ang), jnp.zeros((dk // 2,), F32), batch, ctx)
    return cos, sin


def _sub_table(vec_x, vec_c, nsb):
    b, d = vec_x.shape
    full = jnp.concatenate([jnp.broadcast_to(vec_x[:, None, :], (b, nsb - 1, d)),
                            jnp.broadcast_to(vec_c[None, None, :], (b, 1, d))], axis=1)
    return full.reshape(b * nsb, 1, d)


def _mla_layer(h, xc, gate, norm, tabs, batch, ctx, w_dq, q_g, w_uq, w_dkv, kv_g, w_ukv, w_o):
    cos, sin, partner = tabs
    rq = w_dq.shape[1]
    rkv = kv_g.shape[0]
    heads, nope, rope, dv = MLA_HEADS, MLA_NOPE, MLA_ROPE, MLA_V
    hw = nope + 2 * rope
    w_kr = w_dkv[:, rkv:]
    w_cat = jnp.concatenate([w_dq, w_dkv[:, :rkv], w_kr, w_kr[:, partner]], axis=1).astype(BF16)
    cq, ckv, kr = _mla_down(h, w_cat, q_g, kv_g, cos, sin, rq, rkv)

    scale = float(nope + rope) ** -0.5 * 1.4426950408889634
    wq = w_uq.reshape(rq, heads, nope + rope) * scale
    wq = jnp.concatenate([wq, wq[:, :, nope:][:, :, partner]], axis=2).reshape(rq, heads * hw).astype(BF16)
    q = _mla_q(cq, wq, cos, sin, hw, nope)

    wkv = w_ukv.reshape(rkv, heads, nope + dv)
    wkv = jnp.concatenate([wkv[:, :, :nope].reshape(rkv, heads * nope),
                           wkv[:, :, nope:].reshape(rkv, heads * dv)], axis=1).astype(BF16)
    kv = _matmul(ckv, wkv, BF16)

    o = _attention(q, kv, kr, batch, heads, ctx, hw, nope, dv)
    return _matmul_gated_residual(o, w_o, xc, gate, ctx, norm)


def _ret_layer(h, xc, gate, norm, tabs, batch, ctx, w_q, w_k, w_v, w_gf, w_gb, w_o, decay_f, decay_b):
    cos, sin = tabs
    heads = RET_HEADS
    dk = _weight(w_q)[2] // heads
    q = _ret_qk(h, w_q, cos, sin, dk, 1.0, BF16)
    k = _ret_qk(h, w_k, cos, sin, dk, float(dk) ** -0.5, BF16)
    v = _matmul(h, w_v, BF16)
    gf = _matmul(h, w_gf, BF16, silu=True)
    gb = _matmul(h, w_gb, BF16, silu=True)
    fwd = _ret_chunks(q, k, v, gf, None, _ret_tables(decay_f, False), batch, heads, ctx, False)
    mix = _ret_chunks(q, k, v, gb, fwd, _ret_tables(decay_b, True), batch, heads, ctx, True)
    return _matmul_gated_residual(mix, w_o, xc, gate, ctx, norm)


def kernel(x, c, ctx, c_ctx, mod_w, mod_b, norm_mix_g, norm_ffn_g, mla_w_dq, mla_q_norm_g, mla_w_uq, mla_w_dkv, mla_kv_norm_g, mla_w_ukv, mla_w_o, ret_w_q, ret_w_k, ret_w_v, ret_w_gf, ret_w_gb, ret_w_o, ret_decay_f, ret_decay_b, ffn_w_gate, ffn_w_up, ffn_conv_w, ffn_conv_b, ffn_w_down, final_norm_g):
    batch, seq, d = x.shape
    ctx_len = ctx.shape[1]
    depth = mod_w.shape[0]
    s_len = ctx_len + seq
    nsb = s_len // ctx_len
    assert seq % ctx_len == 0 and ctx_len % RET_CHUNK == 0 and seq % GRID_W == 0
    assert mod_w.shape[2] == N_MOD * d

    cond_rows = 2 * 8
    cond = jnp.zeros((cond_rows, d), F32).at[:batch].set(c).at[batch].set(c_ctx)
    assert batch + 1 <= cond_rows
    mod = _modulation(cond, mod_w, mod_b)

    mla_tabs = _mla_rope_tables(seq, batch, ctx_len)
    ret_tabs = _ret_rope_tables(seq, ret_w_q.shape[2] // RET_HEADS, batch, ctx_len)

    parts = [[_sub_table(mod[l, :batch, k * d:(k + 1) * d], mod[l, batch, k * d:(k + 1) * d], nsb)
              for k in range(N_MOD)] for l in range(depth)]

    bf = lambda w: w.astype(BF16)
    mla_wo, ret_wq, ret_wk, ret_wv = bf(mla_w_o), bf(ret_w_q), bf(ret_w_k), bf(ret_w_v)
    ret_wgf, ret_wgb, ret_wo = bf(ret_w_gf), bf(ret_w_gb), bf(ret_w_o)
    ffn_wg, ffn_wu, ffn_wd = bf(ffn_w_gate), bf(ffn_w_up), bf(ffn_w_down)

    xc, h = _embed_norm(x, ctx, norm_mix_g[0], parts[0][1], parts[0][0])
    for l in range(depth):
        i = l // N_MIXERS
        _, _, g1, sh2, sc2, g2 = parts[l]
        ffn_norm = (norm_ffn_g[l], sc2, sh2)
        if l % N_MIXERS == 0:
            xc, h2 = _mla_layer(h, xc, g1, ffn_norm, mla_tabs, batch, ctx_len, mla_w_dq[i], mla_q_norm_g[i],
                                mla_w_uq[i], mla_w_dkv[i], mla_kv_norm_g[i], mla_w_ukv[i], (mla_wo, i))
        else:
            xc, h2 = _ret_layer(h, xc, g1, ffn_norm, ret_tabs, batch, ctx_len, (ret_wq, i), (ret_wk, i),
                                (ret_wv, i), (ret_wgf, i), (ret_wgb, i), (ret_wo, i),
                                ret_decay_f[i], ret_decay_b[i])
        act = _ffn_gate_up(h2, (ffn_wg, l), (ffn_wu, l), ffn_conv_w[l], ffn_conv_b[l], s_len, ctx_len)
        w_down = (ffn_wd, l)
        if l + 1 < depth:
            xc, h = _matmul_gated_residual(act, w_down, xc, g2, ctx_len,
                                           (norm_mix_g[l + 1], parts[l + 1][1], parts[l + 1][0]))
        else:
            xc = _matmul_gated_residual(act, w_down, xc, g2, ctx_len)

    return _final_norm(xc, final_norm_g, batch, seq, ctx_len).reshape(batch, seq, d)
```
